```python
import jax, jax.numpy as jnp
from jax import lax
import numpy as np

D_MODEL = 2048
BATCH = 2
SEQ = 8192
DEPTH = 4

MLA_HEADS = 8
MLA_NOPE_DIM = 128
MLA_ROPE_DIM = 64
MLA_V_DIM = 128
Q_LORA_RANK = 512
KV_LORA_RANK = 256
MLA_WIDTH = MLA_HEADS * MLA_V_DIM
RET_HEADS = 8
RET_QK_DIM = 128
RET_V_DIM = 128
RET_WIDTH = RET_HEADS * RET_V_DIM
RET_CHUNK = 128
MIX_WIDTH = MLA_WIDTH + RET_WIDTH
D_FF = 4 * D_MODEL
Q_BLOCK = 128
ROPE_BASE = 10000.0
NORM_EPS = 1e-6
GROUPNORM_EPS = 1e-6
IN_WIDTHS = (Q_LORA_RANK, KV_LORA_RANK, MLA_ROPE_DIM,
             RET_HEADS * RET_QK_DIM, RET_HEADS * RET_QK_DIM, RET_WIDTH, RET_WIDTH)
IN_WIDTH = Q_LORA_RANK + KV_LORA_RANK + MLA_ROPE_DIM + 2 * RET_HEADS * RET_QK_DIM + 2 * RET_WIDTH

kernel_name = "hybrid_mla_retention_parallel_heads"


def _split_points():
    pts, acc = [], 0
    for w in IN_WIDTHS[:-1]:
        acc += w
        pts.append(acc)
    return pts


def rmsnorm(x, g):
    xf = x.astype(jnp.float32)
    y = xf * lax.rsqrt(jnp.mean(xf * xf, axis=-1, keepdims=True) + NORM_EPS)
    return (y * g.astype(jnp.float32)).astype(x.dtype)


def rope_tables(positions, dim):
    inv_freq = ROPE_BASE ** (-jnp.arange(0, dim, 2, dtype=jnp.float32) / dim)
    ang = positions.astype(jnp.float32)[..., None] * inv_freq
    return jnp.cos(ang), jnp.sin(ang)


def apply_rope(x, cos, sin):
    half = x.shape[-1] // 2
    x1, x2 = x[..., :half], x[..., half:]
    out = jnp.concatenate([x1 * cos - x2 * sin, x2 * cos + x1 * sin], axis=-1)
    return out.astype(x.dtype)


def mla_attention(c_q, c_kv, k_rope, q_norm_g, kv_norm_g, w_uq, w_ukv, cos, sin):
    B, S, _ = c_q.shape
    H, DN, DR, DV = MLA_HEADS, MLA_NOPE_DIM, MLA_ROPE_DIM, MLA_V_DIM
    q = (rmsnorm(c_q, q_norm_g) @ w_uq).reshape(B, S, H, DN + DR)
    q_nope = q[..., :DN]
    q_rope = apply_rope(q[..., DN:], cos[:, :, None, :], sin[:, :, None, :])
    kv = (rmsnorm(c_kv, kv_norm_g) @ w_ukv).reshape(B, S, H, DN + DV)
    k_nope, v = kv[..., :DN], kv[..., DN:]
    k_r = apply_rope(k_rope, cos, sin)
    scale = (DN + DR) ** -0.5
    nb = S // Q_BLOCK
    qn_blocks = q_nope.reshape(B, nb, Q_BLOCK, H, DN).transpose(1, 0, 2, 3, 4)
    qr_blocks = q_rope.reshape(B, nb, Q_BLOCK, H, DR).transpose(1, 0, 2, 3, 4)
    key_pos = jnp.arange(S)

    def one_block(args):
        qn, qr, blk = args
        s = (jnp.einsum('bqhd,bkhd->bhqk', qn, k_nope)
             + jnp.einsum('bqhr,bkr->bhqk', qr, k_r)).astype(jnp.float32) * scale
        q_pos = blk * Q_BLOCK + jnp.arange(Q_BLOCK)
        mask = key_pos[None, :] <= q_pos[:, None]
        s = jnp.where(mask[None, None], s, -jnp.inf)
        p = jax.nn.softmax(s, axis=-1).astype(v.dtype)
        return jnp.einsum('bhqk,bkhd->bqhd', p, v)

    out = lax.map(one_block, (qn_blocks, qr_blocks, jnp.arange(nb)))
    return out.transpose(1, 0, 2, 3, 4).reshape(B, S, H * DV)


def retention(rq, rk, rv, rg, cos, sin):
    B, S, _ = rq.shape
    H, DK, DV, C = RET_HEADS, RET_QK_DIM, RET_V_DIM, RET_CHUNK
    nc = S // C
    q = apply_rope(rq.reshape(B, S, H, DK), cos[:, :, None, :], sin[:, :, None, :]).astype(jnp.float32) * DK ** -0.5
    k = apply_rope(rk.reshape(B, S, H, DK), cos[:, :, None, :], sin[:, :, None, :]).astype(jnp.float32)
    v = rv.reshape(B, S, H, DV).astype(jnp.float32)
    log_gamma = jnp.log1p(-jnp.exp2(-5.0 - jnp.arange(H, dtype=jnp.float32)))
    idx = jnp.arange(C, dtype=jnp.float32)
    rel = idx[:, None] - idx[None, :]
    decay_intra = jnp.where(rel >= 0, jnp.exp(log_gamma[:, None, None] * jnp.maximum(rel, 0.0)), 0.0)
    xi = jnp.exp(log_gamma[:, None] * (idx + 1.0))
    zeta = jnp.exp(log_gamma[:, None] * (C - 1.0 - idx))
    chunk_decay = jnp.exp(log_gamma * C)

    def to_chunks(t, d):
        return t.reshape(B, nc, C, H, d).transpose(1, 0, 3, 2, 4)

    def step(state, qkv):
        qc, kc, vc = qkv
        scores = jnp.einsum('bhqd,bhkd->bhqk', qc, kc) * decay_intra[None]
        o = (jnp.einsum('bhqk,bhkv->bhqv', scores, vc)
             + jnp.einsum('bhqd,bhdv->bhqv', qc * xi[None, :, :, None], state))
        state = (chunk_decay[None, :, None, None] * state
                 + jnp.einsum('bhkd,bhkv->bhdv', kc * zeta[None, :, :, None], vc))
        return state, o

    state0 = jnp.zeros((B, H, DK, DV), jnp.float32)
    _, o = lax.scan(step, state0, (to_chunks(q, DK), to_chunks(k, DK), to_chunks(v, DV)))
    o = o.transpose(1, 0, 3, 2, 4).reshape(B, S, H, DV)
    mu = jnp.mean(o, axis=-1, keepdims=True)
    var = jnp.mean(jnp.square(o - mu), axis=-1, keepdims=True)
    o = ((o - mu) * lax.rsqrt(var + GROUPNORM_EPS)).reshape(B, S, H * DV)
    return (o * jax.nn.silu(rg.astype(jnp.float32))).astype(rq.dtype)


def setup_inputs(seed: int = 0) -> dict:
    key = jax.random.key(seed)
    ks = jax.random.split(key, 16)
    f32 = jnp.float32

    def w(k, shape, fan_in):
        return jax.random.normal(k, shape, f32) * fan_in ** -0.5

    def gain(k, shape):
        return 1.0 + 0.02 * jax.random.normal(k, shape, f32)

    x = jax.random.normal(ks[0], (BATCH, SEQ, D_MODEL), f32)
    positions = jnp.broadcast_to(jnp.arange(SEQ, dtype=jnp.int32), (BATCH, SEQ))
    return {
        "x": x,
        "positions": positions,
        "attn_norm": gain(ks[1], (DEPTH, D_MODEL)),
        "w_in": w(ks[2], (DEPTH, D_MODEL, IN_WIDTH), D_MODEL),
        "q_norm": gain(ks[3], (DEPTH, Q_LORA_RANK)),
        "kv_norm": gain(ks[4], (DEPTH, KV_LORA_RANK)),
        "w_uq": w(ks[5], (DEPTH, Q_LORA_RANK, MLA_HEADS * (MLA_NOPE_DIM + MLA_ROPE_DIM)), Q_LORA_RANK),
        "w_ukv": w(ks[6], (DEPTH, KV_LORA_RANK, MLA_HEADS * (MLA_NOPE_DIM + MLA_V_DIM)), KV_LORA_RANK),
        "beta_attn": gain(ks[7], (DEPTH, MLA_WIDTH)),
        "beta_ret": gain(ks[8], (DEPTH, RET_WIDTH)),
        "w_o": w(ks[9], (DEPTH, MIX_WIDTH, D_MODEL), MIX_WIDTH),
        "mlp_norm": gain(ks[10], (DEPTH, D_MODEL)),
        "w_up": w(ks[11], (DEPTH, D_MODEL, D_FF), D_MODEL),
        "w_down": w(ks[12], (DEPTH, D_FF, D_MODEL), D_FF),
        "final_norm": gain(ks[13], (D_MODEL,)),
    }


def reference(x, positions, attn_norm, w_in, q_norm, kv_norm, w_uq, w_ukv, beta_attn, beta_ret,
              w_o, mlp_norm, w_up, w_down, final_norm):
    cos64, sin64 = rope_tables(positions, MLA_ROPE_DIM)
    cos128, sin128 = rope_tables(positions, RET_QK_DIM)
    split_pts = _split_points()
    for l in range(DEPTH):
        h = rmsnorm(x, attn_norm[l])
        proj = h @ w_in[l]
        c_q, c_kv, k_rope, rq, rk, rv, rg = jnp.split(proj, split_pts, axis=-1)
        a = mla_attention(c_q, c_kv, k_rope, q_norm[l], kv_norm[l], w_uq[l], w_ukv[l], cos64, sin64)
        r = retention(rq, rk, rv, rg, cos128, sin128)
        mixed = jnp.concatenate([rmsnorm(a, beta_attn[l]), r * beta_ret[l]], axis=-1)
        x = x + mixed @ w_o[l]
        h = rmsnorm(x, mlp_norm[l])
        x = x + jnp.square(jax.nn.relu(h @ w_up[l])) @ w_down[l]
    return rmsnorm(x, final_norm)
```

```python
import functools

import jax
import jax.numpy as jnp
from jax import lax
from jax.experimental import pallas as pl
from jax.experimental.pallas import tpu as pltpu

F32 = jnp.float32
BF16 = jnp.bfloat16

D_MODEL = 2048
DEPTH = 4
MLA_HEADS = 8
NOPE = 128
ROPE = 64
VDIM = 128
Q_LORA = 512
KV_LORA = 256
RET_HEADS = 8
RET_DK = 128
RET_DV = 128
RET_CHUNK = 128
D_FF = 4 * D_MODEL
ROPE_BASE = 10000.0
NORM_EPS = 1e-6
GN_EPS = 1e-6

LANE = 128
QK_PAD = 256
MLA_IN = Q_LORA + KV_LORA + LANE
PROJ_W = MLA_IN + 4 * RET_HEADS * RET_DK
RET_COL0 = MLA_IN // LANE

VMEM_LIMIT = 56 * 1024 * 1024

TM_PROJ = 512
TN_PROJ = PROJ_W // 3
TM_MLA = 512
TQ = 512
TM_RET = 512
TM_OUT = 512
TM_MLP = 512
TF_MLP = 1024
TM_NORM = 1024
TM_TAB = 2048


def _cparams(sem):
    return pltpu.CompilerParams(dimension_semantics=sem, vmem_limit_bytes=VMEM_LIMIT)


def _rms(x, g):
    ms = jnp.mean(x * x, axis=-1, keepdims=True)
    return x * lax.rsqrt(ms + NORM_EPS) * g


def _rope_table_kernel(pos_ref, f128_ref, s128_ref, f64_ref, s64_ref,
                       c128_ref, sn128_ref, c64_ref, sn64_ref):
    pos = pos_ref[...].astype(F32)
    a128 = pos * f128_ref[...]
    c128_ref[...] = jnp.cos(a128)
    sn128_ref[...] = jnp.sin(a128) * s128_ref[...]
    a64 = pos * f64_ref[...]
    c64_ref[...] = jnp.cos(a64)
    sn64_ref[...] = jnp.sin(a64) * s64_ref[...]


def _rope_tables(positions):
    t = positions.size
    pos = positions.reshape(t, 1)
    inv128 = ROPE_BASE ** (-jnp.arange(0, RET_DK, 2, dtype=F32) / RET_DK)
    inv64 = ROPE_BASE ** (-jnp.arange(0, ROPE, 2, dtype=F32) / ROPE)
    f128 = jnp.concatenate([inv128, inv128]).reshape(1, LANE)
    s128 = jnp.concatenate([-jnp.ones(64, F32), jnp.ones(64, F32)]).reshape(1, LANE)
    f64 = jnp.tile(jnp.concatenate([inv64, inv64]), 2).reshape(1, LANE)
    s64 = jnp.tile(jnp.concatenate([-jnp.ones(32, F32), jnp.ones(32, F32)]), 2).reshape(1, LANE)
    row = pl.BlockSpec((1, LANE), lambda i: (0, 0))
    tab = pl.BlockSpec((TM_TAB, LANE), lambda i: (i, 0))
    shp = jax.ShapeDtypeStruct((t, LANE), F32)
    return pl.pallas_call(
        _rope_table_kernel,
        grid=(t // TM_TAB,),
        in_specs=[pl.BlockSpec((TM_TAB, 1), lambda i: (i, 0)), row, row, row, row],
        out_specs=[tab, tab, tab, tab],
        out_shape=[shp, shp, shp, shp],
        compiler_params=_cparams(("parallel",)),
        name="rope_tables",
    )(pos, f128, s128, f64, s64)


def _inproj_kernel(x_ref, g_ref, w_ref, o_ref, h_ref):
    @pl.when(pl.program_id(1) == 0)
    def _():
        h_ref[...] = _rms(x_ref[...], g_ref[...]).astype(BF16)

    o_ref[...] = jnp.dot(h_ref[...], w_ref[...], preferred_element_type=F32).astype(o_ref.dtype)


def _inproj(x2, g, w):
    t = x2.shape[0]
    return pl.pallas_call(
        _inproj_kernel,
        grid=(t // TM_PROJ, PROJ_W // TN_PROJ),
        in_specs=[
            pl.BlockSpec((TM_PROJ, D_MODEL), lambda i, j: (i, 0)),
            pl.BlockSpec((1, D_MODEL), lambda i, j: (0, 0)),
            pl.BlockSpec((D_MODEL, TN_PROJ), lambda i, j: (0, j)),
        ],
        out_specs=pl.BlockSpec((TM_PROJ, TN_PROJ), lambda i, j: (i, j)),
        out_shape=jax.ShapeDtypeStruct((t, PROJ_W), BF16),
        scratch_shapes=[pltpu.VMEM((TM_PROJ, D_MODEL), BF16)],
        compiler_params=_cparams(("parallel", "arbitrary")),
        name="in_proj",
    )(x2, g, w)


def _swap32(x):
    lane = lax.broadcasted_iota(jnp.int32, x.shape, 1)
    return jnp.where((lane & 32) == 0, pltpu.roll(x, 96, 1), pltpu.roll(x, 32, 1))


def _mla_up_kernel(cq_ref, ckv_ref, kr_ref, qg_ref, kvg_ref, wqn_ref, wqr_ref, wk_ref, wvt_ref,
                   c64_ref, s64_ref, q_ref, k_ref, vt_ref):
    scale = (NOPE + ROPE) ** -0.5
    cqn = _rms(cq_ref[...].astype(F32), qg_ref[...]).astype(BF16)
    ckvn = _rms(ckv_ref[...].astype(F32), kvg_ref[...]).astype(BF16)
    cos = c64_ref[...]
    sin = s64_ref[...]
    lane = lax.broadcasted_iota(jnp.int32, cos.shape, 1)
    low = lane < ROPE

    qn = jnp.dot(cqn, wqn_ref[...], preferred_element_type=F32)
    qr = jnp.dot(cqn, wqr_ref[...], preferred_element_type=F32)
    kn = jnp.dot(ckvn, wk_ref[...], preferred_element_type=F32)
    vt = lax.dot_general(wvt_ref[...], ckvn, (((1,), (1,)), ((), ())),
                         preferred_element_type=F32)

    kr = kr_ref[...].astype(F32)
    krr = (kr * cos + _swap32(kr) * sin).astype(BF16)

    for h in range(MLA_HEADS):
        q_ref[:, h * QK_PAD:h * QK_PAD + NOPE] = (qn[:, h * NOPE:(h + 1) * NOPE] * scale).astype(BF16)
        k_ref[:, h * QK_PAD:h * QK_PAD + NOPE] = kn[:, h * NOPE:(h + 1) * NOPE].astype(BF16)
        k_ref[:, h * QK_PAD + NOPE:(h + 1) * QK_PAD] = krr
        vt_ref[h] = vt[h * VDIM:(h + 1) * VDIM, :].astype(BF16)
    for p in range(MLA_HEADS // 2):
        x = qr[:, p * LANE:(p + 1) * LANE]
        xr = (x * cos + _swap32(x) * sin) * scale
        even = jnp.where(low, xr, 0.0)
        odd = jnp.where(low, pltpu.roll(xr, ROPE, 1), 0.0)
        q_ref[:, (2 * p) * QK_PAD + NOPE:(2 * p + 1) * QK_PAD] = even.astype(BF16)
        q_ref[:, (2 * p + 1) * QK_PAD + NOPE:(2 * p + 2) * QK_PAD] = odd.astype(BF16)


def _mla_up(proj3, qg, kvg, wqn, wqr, wk, wvt, c64, s64):
    b, s, _ = proj3.shape
    nc = s // TQ
    full = lambda shape: pl.BlockSpec(shape, lambda bi, i: (0,) * len(shape))
    return pl.pallas_call(
        _mla_up_kernel,
        grid=(b, nc),
        in_specs=[
            pl.BlockSpec((None, TM_MLA, Q_LORA), lambda bi, i: (bi, i, 0)),
            pl.BlockSpec((None, TM_MLA, KV_LORA), lambda bi, i: (bi, i, Q_LORA // KV_LORA)),
            pl.BlockSpec((None, TM_MLA, LANE), lambda bi, i: (bi, i, (Q_LORA + KV_LORA) // LANE)),
            full((1, Q_LORA)), full((1, KV_LORA)),
            full((Q_LORA, MLA_HEADS * NOPE)), full((Q_LORA, MLA_HEADS * ROPE)),
            full((KV_LORA, MLA_HEADS * NOPE)), full((MLA_HEADS * VDIM, KV_LORA)),
            pl.BlockSpec((None, TM_MLA, LANE), lambda bi, i: (bi, i, 0)),
            pl.BlockSpec((None, TM_MLA, LANE), lambda bi, i: (bi, i, 0)),
        ],
        out_specs=[
            pl.BlockSpec((None, TM_MLA, MLA_HEADS * QK_PAD), lambda bi, i: (bi, i, 0)),
            pl.BlockSpec((None, TM_MLA, MLA_HEADS * QK_PAD), lambda bi, i: (bi, i, 0)),
            pl.BlockSpec((None, MLA_HEADS, None, VDIM, TQ), lambda bi, i: (bi, 0, i, 0, 0)),
        ],
        out_shape=[
            jax.ShapeDtypeStruct((b, s, MLA_HEADS * QK_PAD), BF16),
            jax.ShapeDtypeStruct((b, s, MLA_HEADS * QK_PAD), BF16),
            jax.ShapeDtypeStruct((b, MLA_HEADS, nc, VDIM, TQ), BF16),
        ],
        compiler_params=_cparams(("parallel", "parallel")),
        name="mla_up",
    )(proj3, proj3, proj3, qg, kvg, wqn, wqr, wk, wvt, c64, s64)


def _attn_kernel(q_ref, k_ref, vt_ref, o_ref, qt_ref, m_ref, l_ref, acc_ref):
    i = pl.program_id(2)
    qt_ref[...] = q_ref[...].astype(F32).T.astype(BF16)
    m_ref[...] = jnp.full(m_ref.shape, -jnp.inf, F32)
    l_ref[...] = jnp.zeros(l_ref.shape, F32)
    acc_ref[...] = jnp.zeros(acc_ref.shape, F32)

    def chunk(j, masked):
        kc = k_ref[pl.ds(pl.multiple_of(j * TQ, TQ), TQ), :]
        st = jnp.dot(kc, qt_ref[...], preferred_element_type=F32)
        if masked:
            row = lax.broadcasted_iota(jnp.int32, st.shape, 0)
            col = lax.broadcasted_iota(jnp.int32, st.shape, 1)
            st = jnp.where(row <= col, st, -jnp.inf)
        m_old = m_ref[...]
        m_new = jnp.maximum(m_old, jnp.max(st, axis=0, keepdims=True))
        alpha = jnp.exp(m_old - m_new)
        p = jnp.exp(st - m_new)
        l_ref[...] = alpha * l_ref[...] + jnp.sum(p, axis=0, keepdims=True)
        pv = jnp.dot(vt_ref[j], p.astype(BF16), preferred_element_type=F32)
        acc_ref[...] = alpha * acc_ref[...] + pv
        m_ref[...] = m_new

    def body(j, carry):
        chunk(j, False)
        return carry

    lax.fori_loop(0, i, body, 0)
    chunk(i, True)
    o_ref[...] = (acc_ref[...] / l_ref[...]).T


def _attention(q3, k3, vt5):
    b, s, _ = q3.shape
    nq = s // TQ
    return pl.pallas_call(
        _attn_kernel,
        grid=(b, MLA_HEADS, nq),
        in_specs=[
            pl.BlockSpec((None, TQ, QK_PAD), lambda bi, h, i: (bi, i, h)),
            pl.BlockSpec((None, s, QK_PAD), lambda bi, h, i: (bi, 0, h)),
            pl.BlockSpec((None, None, nq, VDIM, TQ), lambda bi, h, i: (bi, h, 0, 0, 0)),
        ],
        out_specs=pl.BlockSpec((None, TQ, VDIM), lambda bi, h, i: (bi, i, h)),
        out_shape=jax.ShapeDtypeStruct((b, s, MLA_HEADS * VDIM), F32),
        scratch_shapes=[
            pltpu.VMEM((QK_PAD, TQ), BF16),
            pltpu.VMEM((1, TQ), F32),
            pltpu.VMEM((1, TQ), F32),
            pltpu.VMEM((VDIM, TQ), F32),
        ],
        compiler_params=_cparams(("parallel", "parallel", "arbitrary")),
        name="mla_attention",
    )(q3, k3, vt5)


def _ret_kernel(rq_ref, rk_ref, rv_ref, rg_ref, c_ref, s_ref, dec_ref, xi_ref, zeta_ref, cd_ref, beta_ref,
                o_ref, state_ref):
    @pl.when(pl.program_id(2) == 0)
    def _():
        state_ref[...] = jnp.zeros(state_ref.shape, F32)

    dec = dec_ref[...]
    xi = xi_ref[...]
    zeta = zeta_ref[...]
    cd = cd_ref[...]
    outs = []
    for c in range(TM_RET // RET_CHUNK):
        rows = slice(c * RET_CHUNK, (c + 1) * RET_CHUNK)
        cos = c_ref[rows, :]
        sin = s_ref[rows, :]
        rq = rq_ref[rows, :].astype(F32)
        rk = rk_ref[rows, :].astype(F32)
        v = rv_ref[rows, :]
        q = (rq * cos + pltpu.roll(rq, RET_DK // 2, 1) * sin) * (RET_DK ** -0.5)
        k = rk * cos + pltpu.roll(rk, RET_DK // 2, 1) * sin
        scores = lax.dot_general(q.astype(BF16), k.astype(BF16), (((1,), (1,)), ((), ())),
                                 preferred_element_type=F32) * dec
        st = state_ref[...]
        o = (jnp.dot(scores.astype(BF16), v, preferred_element_type=F32)
             + jnp.dot((q * xi).astype(BF16), st.astype(BF16), preferred_element_type=F32))
        kz = (k * zeta).astype(BF16)
        state_ref[...] = cd * st + lax.dot_general(kz, v, (((0,), (0,)), ((), ())),
                                                   preferred_element_type=F32)
        outs.append(o)
    o = jnp.concatenate(outs, axis=0)
    mu = jnp.mean(o, axis=-1, keepdims=True)
    d = o - mu
    var = jnp.mean(d * d, axis=-1, keepdims=True)
    on = d * lax.rsqrt(var + GN_EPS)
    g = rg_ref[...].astype(F32)
    silu = g * (1.0 / (1.0 + jnp.exp(-g)))
    o_ref[...] = (on * silu * beta_ref[...]).astype(o_ref.dtype)


def _retention(proj3, c128, s128, dec, xi, zeta, cd, beta):
    b, s, _ = proj3.shape
    seg = lambda k: pl.BlockSpec((None, TM_RET, RET_DK), lambda bi, h, i: (bi, i, RET_COL0 + k * RET_HEADS + h))
    tab = pl.BlockSpec((None, TM_RET, LANE), lambda bi, h, i: (bi, i, 0))
    per_head = pl.BlockSpec((None, RET_CHUNK, LANE), lambda bi, h, i: (h, 0, 0))
    return pl.pallas_call(
        _ret_kernel,
        grid=(b, RET_HEADS, s // TM_RET),
        in_specs=[seg(0), seg(1), seg(2), seg(3), tab, tab, per_head, per_head, per_head, per_head,
                  pl.BlockSpec((1, RET_DV), lambda bi, h, i: (0, h))],
        out_specs=pl.BlockSpec((None, TM_RET, RET_DV), lambda bi, h, i: (bi, i, h)),
        out_shape=jax.ShapeDtypeStruct((b, s, RET_HEADS * RET_DV), BF16),
        scratch_shapes=[pltpu.VMEM((RET_DK, RET_DV), F32)],
        compiler_params=_cparams(("parallel", "parallel", "arbitrary")),
        name="retention",
    )(proj3, proj3, proj3, proj3, c128, s128, dec, xi, zeta, cd, beta)


def _retention_constants():
    h, c = RET_HEADS, RET_CHUNK
    log_gamma = jnp.log1p(-jnp.exp2(-5.0 - jnp.arange(h, dtype=F32)))
    idx = jnp.arange(c, dtype=F32)
    rel = idx[:, None] - idx[None, :]
    dec = jnp.where(rel >= 0, jnp.exp(log_gamma[:, None, None] * jnp.maximum(rel, 0.0)), 0.0)
    xi = jnp.exp(log_gamma[:, None] * (idx + 1.0))
    zeta = jnp.exp(log_gamma[:, None] * (c - 1.0 - idx))
    cd = jnp.exp(log_gamma * c)
    bc = lambda t: jnp.broadcast_to(t[:, :, None], (h, c, LANE))
    return dec, bc(xi), bc(zeta), jnp.broadcast_to(cd[:, None, None], (h, c, LANE))


def _outproj_kernel(a_ref, r_ref, x_ref, ba_ref, wt_ref, wb_ref, o_ref):
    an = _rms(a_ref[...], ba_ref[...]).astype(BF16)
    y = jnp.dot(an, wt_ref[...], preferred_element_type=F32)
    y = y + jnp.dot(r_ref[...], wb_ref[...], preferred_element_type=F32)
    o_ref[...] = x_ref[...] + y


def _outproj(a2, r2, x2, ba, wo):
    t = x2.shape[0]
    half = MLA_HEADS * VDIM
    return pl.pallas_call(
        _outproj_kernel,
        grid=(t // TM_OUT,),
        in_specs=[
            pl.BlockSpec((TM_OUT, half), lambda i: (i, 0)),
            pl.BlockSpec((TM_OUT, half), lambda i: (i, 0)),
            pl.BlockSpec((TM_OUT, D_MODEL), lambda i: (i, 0)),
            pl.BlockSpec((1, half), lambda i: (0, 0)),
            pl.BlockSpec((half, D_MODEL), lambda i: (0, 0)),
            pl.BlockSpec((half, D_MODEL), lambda i: (1, 0)),
        ],
        out_specs=pl.BlockSpec((TM_OUT, D_MODEL), lambda i: (i, 0)),
        out_shape=jax.ShapeDtypeStruct((t, D_MODEL), F32),
        compiler_params=_cparams(("parallel",)),
        name="out_proj",
    )(a2, r2, x2, ba, wo, wo)


def _mlp_kernel(x_ref, g_ref, wu_ref, wd_ref, o_ref, h_ref):
    @pl.when(pl.program_id(1) == 0)
    def _():
        x = x_ref[...]
        h_ref[...] = _rms(x, g_ref[...]).astype(BF16)
        o_ref[...] = x

    u = jnp.maximum(jnp.dot(h_ref[...], wu_ref[...], preferred_element_type=F32), 0.0)
    u = (u * u).astype(BF16)
    o_ref[...] += jnp.dot(u, wd_ref[...], preferred_element_type=F32)


def _mlp(x2, g, wu, wd):
    t = x2.shape[0]
    return pl.pallas_call(
        _mlp_kernel,
        grid=(t // TM_MLP, D_FF // TF_MLP),
        in_specs=[
            pl.BlockSpec((TM_MLP, D_MODEL), lambda i, j: (i, 0)),
            pl.BlockSpec((1, D_MODEL), lambda i, j: (0, 0)),
            pl.BlockSpec((D_MODEL, TF_MLP), lambda i, j: (0, j)),
            pl.BlockSpec((TF_MLP, D_MODEL), lambda i, j: (j, 0)),
        ],
        out_specs=pl.BlockSpec((TM_MLP, D_MODEL), lambda i, j: (i, 0)),
        out_shape=jax.ShapeDtypeStruct((t, D_MODEL), F32),
        scratch_shapes=[pltpu.VMEM((TM_MLP, D_MODEL), BF16)],
        compiler_params=_cparams(("parallel", "arbitrary")),
        name="mlp",
    )(x2, g, wu, wd)


def _norm_kernel(x_ref, g_ref, o_ref):
    o_ref[...] = _rms(x_ref[...], g_ref[...])


def _final_norm(x2, g):
    t = x2.shape[0]
    return pl.pallas_call(
        _norm_kernel,
        grid=(t // TM_NORM,),
        in_specs=[pl.BlockSpec((TM_NORM, D_MODEL), lambda i: (i, 0)),
                  pl.BlockSpec((1, D_MODEL), lambda i: (0, 0))],
        out_specs=pl.BlockSpec((TM_NORM, D_MODEL), lambda i: (i, 0)),
        out_shape=jax.ShapeDtypeStruct((t, D_MODEL), F32),
        compiler_params=_cparams(("parallel",)),
        name="final_norm",
    )(x2, g)


def _prep_w_in(w):
    cut = Q_LORA + KV_LORA + ROPE
    pad = jnp.zeros((w.shape[0], LANE - ROPE), w.dtype)
    return jnp.concatenate([w[:, :cut], pad, w[:, cut:]], axis=1).astype(BF16)


def _prep_w_uq(w):
    w3 = w.reshape(Q_LORA, MLA_HEADS, NOPE + ROPE)
    wqn = w3[:, :, :NOPE].reshape(Q_LORA, MLA_HEADS * NOPE).astype(BF16)
    wqr = w3[:, :, NOPE:].reshape(Q_LORA, MLA_HEADS * ROPE).astype(BF16)
    return wqn, wqr


def _prep_w_ukv(w):
    w3 = w.reshape(KV_LORA, MLA_HEADS, NOPE + VDIM)
    wk = w3[:, :, :NOPE].reshape(KV_LORA, MLA_HEADS * NOPE).astype(BF16)
    wvt = w3[:, :, NOPE:].reshape(KV_LORA, MLA_HEADS * VDIM).T.astype(BF16)
    return wk, wvt


def kernel(x, positions, attn_norm, w_in, q_norm, kv_norm, w_uq, w_ukv, beta_attn, beta_ret,
           w_o, mlp_norm, w_up, w_down, final_norm):
    b, s, d = x.shape
    t = b * s
    c128, s128, c64, s64 = _rope_tables(positions)
    c128, s128, c64, s64 = (a.reshape(b, s, LANE) for a in (c128, s128, c64, s64))
    dec, xi, zeta, cd = _retention_constants()

    x2 = x.reshape(t, d)
    for l in range(DEPTH):
        proj = _inproj(x2, attn_norm[l].reshape(1, d), _prep_w_in(w_in[l]))
        proj3 = proj.reshape(b, s, PROJ_W)
        wqn, wqr = _prep_w_uq(w_uq[l])
        wk, wvt = _prep_w_ukv(w_ukv[l])
        q3, k3, vt5 = _mla_up(proj3, q_norm[l].reshape(1, Q_LORA), kv_norm[l].reshape(1, KV_LORA),
                              wqn, wqr, wk, wvt, c64, s64)
        a3 = _attention(q3, k3, vt5)
        r3 = _retention(proj3, c128, s128, dec, xi, zeta, cd, beta_ret[l].reshape(1, -1))
        x2 = _outproj(a3.reshape(t, -1), r3.reshape(t, -1), x2, beta_attn[l].reshape(1, -1),
                      w_o[l].astype(BF16))
        x2 = _mlp(x2, mlp_norm[l].reshape(1, d), w_up[l].astype(BF16), w_down[l].astype(BF16))
    return _final_norm(x2, final_norm.reshape(1, d)).reshape(b, s, d)
```

```python
import functools

import jax
import jax.numpy as jnp
from jax import lax
from jax.experimental import pallas as pl
from jax.experimental.pallas import tpu as pltpu

F32 = jnp.float32
BF16 = jnp.bfloat16

D_MODEL = 2048
DEPTH = 4
MLA_HEADS = 8
NOPE = 128
ROPE = 64
VDIM = 128
Q_LORA = 512
KV_LORA = 256
RET_HEADS = 8
RET_DK = 128
RET_DV = 128
D_FF = 4 * D_MODEL
ROPE_BASE = 10000.0
NORM_EPS = 1e-6
GN_EPS = 1e-6
LOG2E = 1.4426950408889634

LANE = 128
QK_PAD = 256
MXU_N = 256
MLA_IN = 1024
PROJ_W = MLA_IN + 4 * RET_HEADS * RET_DK
RET_COL0 = MLA_IN // LANE

VMEM_LIMIT = 56 * 1024 * 1024

TM_PROJ = 1024
TN_PROJ = 5 * MXU_N
TK = 512
TM_MLA = TK
TQ = 1024
TQ_SUB = 256
TM_RET = 512
RET_GROUP = 8
TM_OUT = 512
TM_MLP = 512
TF_MLP = 1024
TM_NORM = 1024
TM_TAB = 2048


def _cparams(sem):
    return pltpu.CompilerParams(dimension_semantics=sem, vmem_limit_bytes=VMEM_LIMIT)


def _rms(x, g):
    ms = jnp.mean(x * x, axis=-1, keepdims=True)
    return x * lax.rsqrt(ms + NORM_EPS) * g


def _rope_table_kernel(pos_ref, f128_ref, s128_ref, f64_ref, s64_ref,
                       c128_ref, sn128_ref, c64_ref, sn64_ref):
    pos = pos_ref[...].astype(F32)
    a128 = pos * f128_ref[...]
    c128_ref[...] = jnp.cos(a128)
    sn128_ref[...] = jnp.sin(a128) * s128_ref[...]
    a64 = pos * f64_ref[...]
    c64_ref[...] = jnp.cos(a64)
    sn64_ref[...] = jnp.sin(a64) * s64_ref[...]


def _rope_tables(positions):
    t = positions.size
    pos = positions.reshape(t, 1)
    inv128 = ROPE_BASE ** (-jnp.arange(0, RET_DK, 2, dtype=F32) / RET_DK)
    inv64 = ROPE_BASE ** (-jnp.arange(0, ROPE, 2, dtype=F32) / ROPE)
    f128 = jnp.concatenate([inv128, inv128]).reshape(1, LANE)
    s128 = jnp.concatenate([-jnp.ones(64, F32), jnp.ones(64, F32)]).reshape(1, LANE)
    f64 = jnp.tile(jnp.concatenate([inv64, inv64]), 2).reshape(1, LANE)
    s64 = jnp.tile(jnp.concatenate([-jnp.ones(32, F32), jnp.ones(32, F32)]), 2).reshape(1, LANE)
    row = pl.BlockSpec((1, LANE), lambda i: (0, 0))
    tab = pl.BlockSpec((TM_TAB, LANE), lambda i: (i, 0))
    shp = jax.ShapeDtypeStruct((t, LANE), F32)
    return pl.pallas_call(
        _rope_table_kernel,
        grid=(t // TM_TAB,),
        in_specs=[pl.BlockSpec((TM_TAB, 1), lambda i: (i, 0)), row, row, row, row],
        out_specs=[tab, tab, tab, tab],
        out_shape=[shp, shp, shp, shp],
        compiler_params=_cparams(("parallel",)),
        name="rope_tables",
    )(pos, f128, s128, f64, s64)


def _inproj_kernel(x_ref, g_ref, w_ref, o_ref, h_ref):
    @pl.when(pl.program_id(1) == 0)
    def _():
        h_ref[...] = _rms(x_ref[...], g_ref[...]).astype(BF16)

    o_ref[...] = jnp.dot(h_ref[...], w_ref[...], preferred_element_type=F32).astype(o_ref.dtype)


def _inproj(x2, g, w):
    t = x2.shape[0]
    return pl.pallas_call(
        _inproj_kernel,
        grid=(t // TM_PROJ, PROJ_W // TN_PROJ),
        in_specs=[
            pl.BlockSpec((TM_PROJ, D_MODEL), lambda i, j: (i, 0)),
            pl.BlockSpec((1, D_MODEL), lambda i, j: (0, 0)),
            pl.BlockSpec((D_MODEL, TN_PROJ), lambda i, j: (0, j)),
        ],
        out_specs=pl.BlockSpec((TM_PROJ, TN_PROJ), lambda i, j: (i, j)),
        out_shape=jax.ShapeDtypeStruct((t, PROJ_W), BF16),
        scratch_shapes=[pltpu.VMEM((TM_PROJ, D_MODEL), BF16)],
        compiler_params=_cparams(("parallel", "arbitrary")),
        name="in_proj",
    )(x2, g, w)


def _swap32(x):
    lane = lax.broadcasted_iota(jnp.int32, x.shape, 1)
    return jnp.where((lane & 32) == 0, pltpu.roll(x, 96, 1), pltpu.roll(x, 32, 1))


def _mla_up_kernel(cq_ref, ckv_ref, kr_ref, qg_ref, kvg_ref, wqn_ref, wqr_ref, wk_ref, wvt_ref,
                   c64_ref, s64_ref, q_ref, k_ref, vt_ref):
    scale = (NOPE + ROPE) ** -0.5 * LOG2E
    cqn = _rms(cq_ref[...].astype(F32), qg_ref[...]).astype(BF16)
    ckvn = _rms(ckv_ref[...].astype(F32), kvg_ref[...]).astype(BF16)
    cos = c64_ref[...]
    sin = s64_ref[...]
    lane = lax.broadcasted_iota(jnp.int32, cos.shape, 1)
    low = lane < ROPE

    qn = jnp.dot(cqn, wqn_ref[...], preferred_element_type=F32)
    qr = jnp.dot(cqn, wqr_ref[...], preferred_element_type=F32)
    kn = jnp.dot(ckvn, wk_ref[...], preferred_element_type=F32)
    vt = lax.dot_general(wvt_ref[...], ckvn, (((1,), (1,)), ((), ())),
                         preferred_element_type=F32)

    kr = kr_ref[...].astype(F32)
    krr = (kr * cos + _swap32(kr) * sin).astype(BF16)

    for h in range(MLA_HEADS):
        q_ref[:, h * QK_PAD:h * QK_PAD + NOPE] = (qn[:, h * NOPE:(h + 1) * NOPE] * scale).astype(BF16)
        k_ref[:, h * QK_PAD:h * QK_PAD + NOPE] = kn[:, h * NOPE:(h + 1) * NOPE].astype(BF16)
        k_ref[:, h * QK_PAD + NOPE:(h + 1) * QK_PAD] = krr
        vt_ref[h] = vt[h * VDIM:(h + 1) * VDIM, :].astype(BF16)
    for p in range(MLA_HEADS // 2):
        x = qr[:, p * LANE:(p + 1) * LANE]
        xr = (x * cos + _swap32(x) * sin) * scale
        even = jnp.where(low, xr, 0.0)
        odd = jnp.where(low, pltpu.roll(xr, ROPE, 1), 0.0)
        q_ref[:, (2 * p) * QK_PAD + NOPE:(2 * p + 1) * QK_PAD] = even.astype(BF16)
        q_ref[:, (2 * p + 1) * QK_PAD + NOPE:(2 * p + 2) * QK_PAD] = odd.astype(BF16)


def _mla_up(proj3, qg, kvg, wqn, wqr, wk, wvt, c64, s64):
    b, s, _ = proj3.shape
    nc = s // TK
    full = lambda shape: pl.BlockSpec(shape, lambda bi, i: (0,) * len(shape))
    return pl.pallas_call(
        _mla_up_kernel,
        grid=(b, nc),
        in_specs=[
            pl.BlockSpec((None, TM_MLA, Q_LORA), lambda bi, i: (bi, i, 0)),
            pl.BlockSpec((None, TM_MLA, KV_LORA), lambda bi, i: (bi, i, Q_LORA // KV_LORA)),
            pl.BlockSpec((None, TM_MLA, LANE), lambda bi, i: (bi, i, (Q_LORA + KV_LORA) // LANE)),
            full((1, Q_LORA)), full((1, KV_LORA)),
            full((Q_LORA, MLA_HEADS * NOPE)), full((Q_LORA, MLA_HEADS * ROPE)),
            full((KV_LORA, MLA_HEADS * NOPE)), full((MLA_HEADS * VDIM, KV_LORA)),
            pl.BlockSpec((None, TM_MLA, LANE), lambda bi, i: (bi, i, 0)),
            pl.BlockSpec((None, TM_MLA, LANE), lambda bi, i: (bi, i, 0)),
        ],
        out_specs=[
            pl.BlockSpec((None, TM_MLA, MLA_HEADS * QK_PAD), lambda bi, i: (bi, i, 0)),
            pl.BlockSpec((None, TM_MLA, MLA_HEADS * QK_PAD), lambda bi, i: (bi, i, 0)),
            pl.BlockSpec((None, MLA_HEADS, None, VDIM, TK), lambda bi, i: (bi, 0, i, 0, 0)),
        ],
        out_shape=[
            jax.ShapeDtypeStruct((b, s, MLA_HEADS * QK_PAD), BF16),
            jax.ShapeDtypeStruct((b, s, MLA_HEADS * QK_PAD), BF16),
            jax.ShapeDtypeStruct((b, MLA_HEADS, nc, VDIM, TK), BF16),
        ],
        compiler_params=_cparams(("parallel", "parallel")),
        name="mla_up",
    )(proj3, proj3, proj3, qg, kvg, wqn, wqr, wk, wvt, c64, s64)


def _attn_kernel(q_ref, k_ref, vt_ref, o_ref, qt_ref, s_ref, m_ref, l_ref, acc_ref):
    i = pl.program_id(2)
    qt_ref[...] = q_ref[...].astype(F32).T.astype(BF16)
    m_ref[...] = jnp.full(m_ref.shape, -jnp.inf, F32)
    l_ref[...] = jnp.zeros(l_ref.shape, F32)
    acc_ref[...] = jnp.zeros(acc_ref.shape, F32)

    def scores(j, slot, q_lo=0):
        kc = k_ref[pl.ds(pl.multiple_of(j * TK, TK), TK), :]
        s_ref[slot, :, q_lo:] = jnp.dot(kc, qt_ref[:, q_lo:], preferred_element_type=F32)

    def update(j, slot, diag):
        for c in range(TQ // TQ_SUB):
            q_lo = c * TQ_SUB
            cols = slice(q_lo, q_lo + TQ_SUB)
            nk, masked, k_lo = TK, False, 0
            if diag is not None:
                k_lo = diag * TK
                if k_lo >= q_lo + TQ_SUB:
                    continue
                nk = min(TK, q_lo + TQ_SUB - k_lo)
                masked = k_lo + nk - 1 > q_lo
            st = s_ref[slot, :nk, cols]
            if masked:
                row = lax.broadcasted_iota(jnp.int32, st.shape, 0) + k_lo
                col = lax.broadcasted_iota(jnp.int32, st.shape, 1) + q_lo
                st = jnp.where(row <= col, st, -jnp.inf)
            m_old = m_ref[:, cols]
            m_new = jnp.maximum(m_old, jnp.max(st, axis=0, keepdims=True))
            alpha = jnp.exp2(m_old - m_new)
            p = jnp.exp2(st - m_new)
            l_ref[:, cols] = alpha * l_ref[:, cols] + jnp.sum(p, axis=0, keepdims=True)
            pv = jnp.dot(vt_ref[j][:, :nk], p.astype(BF16), preferred_element_type=F32)
            acc_ref[:, cols] = alpha * acc_ref[:, cols] + pv
            m_ref[:, cols] = m_new

    band = TQ // TK
    first_diag = band * i
    scores(0, 0)

    def body(t, carry):
        j = 2 * t
        scores(j + 1, 1)
        update(j, 0, None)
        scores(j + 2, 0)
        update(j + 1, 1, None)
        return carry

    lax.fori_loop(0, i * (band // 2), body, 0)

    for d in range(band):
        if d + 1 < band:
            scores(first_diag + d + 1, (d + 1) % 2, (d + 1) * TK)
        update(first_diag + d, d % 2, d)

    o_ref[...] = (acc_ref[...] / l_ref[...]).T


def _attention(q3, k3, vt5):
    b, s, _ = q3.shape
    nq = s // TQ
    return pl.pallas_call(
        _attn_kernel,
        grid=(b, MLA_HEADS, nq),
        in_specs=[
            pl.BlockSpec((None, TQ, QK_PAD), lambda bi, h, i: (bi, i, h)),
            pl.BlockSpec((None, s, QK_PAD), lambda bi, h, i: (bi, 0, h)),
            pl.BlockSpec((None, None, s // TK, VDIM, TK), lambda bi, h, i: (bi, h, 0, 0, 0)),
        ],
        out_specs=pl.BlockSpec((None, TQ, VDIM), lambda bi, h, i: (bi, i, h)),
        out_shape=jax.ShapeDtypeStruct((b, s, MLA_HEADS * VDIM), F32),
        scratch_shapes=[
            pltpu.VMEM((QK_PAD, TQ), BF16),
            pltpu.VMEM((2, TK, TQ), F32),
            pltpu.VMEM((1, TQ), F32),
            pltpu.VMEM((1, TQ), F32),
            pltpu.VMEM((VDIM, TQ), F32),
        ],
        compiler_params=_cparams(("parallel", "parallel", "arbitrary")),
        name="mla_attention",
    )(q3, k3, vt5)


def _ret_kernel(rq_ref, rk_ref, rv_ref, rg_ref, c_ref, s_ref, dec_ref, xi_ref, zeta_ref, cd_ref, beta_ref,
                o_ref, state_ref):
    @pl.when(pl.program_id(2) == 0)
    def _():
        state_ref[...] = jnp.zeros(state_ref.shape, F32)

    cos = c_ref[...]
    sin = s_ref[...]
    for g in range(RET_GROUP):
        cols = slice(g * RET_DK, (g + 1) * RET_DK)
        rq = rq_ref[:, cols].astype(F32)
        rk = rk_ref[:, cols].astype(F32)
        v = rv_ref[:, cols]
        q = (rq * cos + pltpu.roll(rq, RET_DK // 2, 1) * sin) * (RET_DK ** -0.5)
        k = rk * cos + pltpu.roll(rk, RET_DK // 2, 1) * sin
        scores = lax.dot_general(q.astype(BF16), k.astype(BF16), (((1,), (1,)), ((), ())),
                                 preferred_element_type=F32) * dec_ref[g]
        st = state_ref[g]
        o = (jnp.dot(scores.astype(BF16), v, preferred_element_type=F32)
             + jnp.dot((q * xi_ref[g]).astype(BF16), st.astype(BF16), preferred_element_type=F32))
        kz = (k * zeta_ref[g]).astype(BF16)
        state_ref[g] = cd_ref[g] * st + lax.dot_general(kz, v, (((0,), (0,)), ((), ())),
                                                        preferred_element_type=F32)
        mu = jnp.mean(o, axis=-1, keepdims=True)
        d = o - mu
        var = jnp.mean(d * d, axis=-1, keepdims=True)
        on = d * lax.rsqrt(var + GN_EPS)
        gate = rg_ref[:, cols].astype(F32)
        silu = gate * (1.0 / (1.0 + jnp.exp(-gate)))
        o_ref[:, cols] = (on * silu * beta_ref[:, cols]).astype(o_ref.dtype)


def _retention(proj3, c128, s128, dec, xi, zeta, cd, beta):
    b, s, _ = proj3.shape
    gw = RET_GROUP * RET_DK
    ngroups = RET_HEADS // RET_GROUP
    seg = lambda k: pl.BlockSpec((None, TM_RET, gw),
                                 lambda bi, h, i: (bi, i, (RET_COL0 + k * RET_HEADS) // RET_GROUP + h))
    tab = pl.BlockSpec((None, TM_RET, LANE), lambda bi, h, i: (bi, i, 0))
    per_head = lambda rows, width: pl.BlockSpec((RET_GROUP, rows, width), lambda bi, h, i: (h, 0, 0))
    return pl.pallas_call(
        _ret_kernel,
        grid=(b, ngroups, s // TM_RET),
        in_specs=[seg(0), seg(1), seg(2), seg(3), tab, tab,
                  per_head(TM_RET, TM_RET), per_head(TM_RET, LANE), per_head(TM_RET, LANE), per_head(RET_DK, LANE),
                  pl.BlockSpec((1, gw), lambda bi, h, i: (0, h))],
        out_specs=pl.BlockSpec((None, TM_RET, gw), lambda bi, h, i: (bi, i, h)),
        out_shape=jax.ShapeDtypeStruct((b, s, RET_HEADS * RET_DV), BF16),
        scratch_shapes=[pltpu.VMEM((RET_GROUP, RET_DK, RET_DV), F32)],
        compiler_params=_cparams(("parallel", "parallel", "arbitrary")),
        name="retention",
    )(proj3, proj3, proj3, proj3, c128, s128, dec, xi, zeta, cd, beta)


def _retention_constants():
    h, c = RET_HEADS, TM_RET
    log_gamma = jnp.log1p(-jnp.exp2(-5.0 - jnp.arange(h, dtype=F32)))
    idx = jnp.arange(c, dtype=F32)
    rel = idx[:, None] - idx[None, :]
    dec = jnp.where(rel >= 0, jnp.exp(log_gamma[:, None, None] * jnp.maximum(rel, 0.0)), 0.0)
    xi = jnp.exp(log_gamma[:, None] * (idx + 1.0))
    zeta = jnp.exp(log_gamma[:, None] * (c - 1.0 - idx))
    cd = jnp.exp(log_gamma * c)
    bc = lambda t: jnp.broadcast_to(t[:, :, None], (h, c, LANE))
    return dec, bc(xi), bc(zeta), jnp.broadcast_to(cd[:, None, None], (h, RET_DK, LANE))


def _outproj_kernel(a_ref, r_ref, x_ref, ba_ref, wt_ref, wb_ref, o_ref):
    an = _rms(a_ref[...], ba_ref[...]).astype(BF16)
    y = jnp.dot(an, wt_ref[...], preferred_element_type=F32)
    y = y + jnp.dot(r_ref[...], wb_ref[...], preferred_element_type=F32)
    o_ref[...] = x_ref[...] + y


def _outproj(a2, r2, x2, ba, wo):
    t = x2.shape[0]
    half = MLA_HEADS * VDIM
    return pl.pallas_call(
        _outproj_kernel,
        grid=(t // TM_OUT,),
        in_specs=[
            pl.BlockSpec((TM_OUT, half), lambda i: (i, 0)),
            pl.BlockSpec((TM_OUT, half), lambda i: (i, 0)),
            pl.BlockSpec((TM_OUT, D_MODEL), lambda i: (i, 0)),
            pl.BlockSpec((1, half), lambda i: (0, 0)),
            pl.BlockSpec((half, D_MODEL), lambda i: (0, 0)),
            pl.BlockSpec((half, D_MODEL), lambda i: (1, 0)),
        ],
        out_specs=pl.BlockSpec((TM_OUT, D_MODEL), lambda i: (i, 0)),
        out_shape=jax.ShapeDtypeStruct((t, D_MODEL), F32),
        compiler_params=_cparams(("parallel",)),
        name="out_proj",
    )(a2, r2, x2, ba, wo, wo)


def _mlp_kernel(x_ref, g_ref, wu_ref, wd_ref, o_ref, h_ref):
    @pl.when(pl.program_id(1) == 0)
    def _():
        x = x_ref[...]
        h_ref[...] = _rms(x, g_ref[...]).astype(BF16)
        o_ref[...] = x

    u = jnp.maximum(jnp.dot(h_ref[...], wu_ref[...], preferred_element_type=F32), 0.0)
    u = (u * u).astype(BF16)
    o_ref[...] += jnp.dot(u, wd_ref[...], preferred_element_type=F32)


def _mlp(x2, g, wu, wd):
    t = x2.shape[0]
    return pl.pallas_call(
        _mlp_kernel,
        grid=(t // TM_MLP, D_FF // TF_MLP),
        in_specs=[
            pl.BlockSpec((TM_MLP, D_MODEL), lambda i, j: (i, 0)),
            pl.BlockSpec((1, D_MODEL), lambda i, j: (0, 0)),
            pl.BlockSpec((D_MODEL, TF_MLP), lambda i, j: (0, j)),
            pl.BlockSpec((TF_MLP, D_MODEL), lambda i, j: (j, 0)),
        ],
        out_specs=pl.BlockSpec((TM_MLP, D_MODEL), lambda i, j: (i, 0)),
        out_shape=jax.ShapeDtypeStruct((t, D_MODEL), F32),
        scratch_shapes=[pltpu.VMEM((TM_MLP, D_MODEL), BF16)],
        compiler_params=_cparams(("parallel", "arbitrary")),
        name="mlp",
    )(x2, g, wu, wd)


def _norm_kernel(x_ref, g_ref, o_ref):
    o_ref[...] = _rms(x_ref[...], g_ref[...])


def _final_norm(x2, g):
    t = x2.shape[0]
    return pl.pallas_call(
        _norm_kernel,
        grid=(t // TM_NORM,),
        in_specs=[pl.BlockSpec((TM_NORM, D_MODEL), lambda i: (i, 0)),
                  pl.BlockSpec((1, D_MODEL), lambda i: (0, 0))],
        out_specs=pl.BlockSpec((TM_NORM, D_MODEL), lambda i: (i, 0)),
        out_shape=jax.ShapeDtypeStruct((t, D_MODEL), F32),
        compiler_params=_cparams(("parallel",)),
        name="final_norm",
    )(x2, g)


def _prep_w_in(w):
    cut = Q_LORA + KV_LORA + ROPE
    pad = jnp.zeros((w.shape[0], MLA_IN - cut), w.dtype)
    return jnp.concatenate([w[:, :cut], pad, w[:, cut:]], axis=1).astype(BF16)


def _prep_w_uq(w):
    w3 = w.reshape(Q_LORA, MLA_HEADS, NOPE + ROPE)
    wqn = w3[:, :, :NOPE].reshape(Q_LORA, MLA_HEADS * NOPE).astype(BF16)
    wqr = w3[:, :, NOPE:].reshape(Q_LORA, MLA_HEADS * ROPE).astype(BF16)
    return wqn, wqr


def _prep_w_ukv(w):
    w3 = w.reshape(KV_LORA, MLA_HEADS, NOPE + VDIM)
    wk = w3[:, :, :NOPE].reshape(KV_LORA, MLA_HEADS * NOPE).astype(BF16)
    wvt = w3[:, :, NOPE:].reshape(KV_LORA, MLA_HEADS * VDIM).T.astype(BF16)
    return wk, wvt


def kernel(x, positions, attn_norm, w_in, q_norm, kv_norm, w_uq, w_ukv, beta_attn, beta_ret,
           w_o, mlp_norm, w_up, w_down, final_norm):
    b, s, d = x.shape
    t = b * s
    c128, s128, c64, s64 = _rope_tables(positions)
    c128, s128, c64, s64 = (a.reshape(b, s, LANE) for a in (c128, s128, c64, s64))
    dec, xi, zeta, cd = _retention_constants()

    x2 = x.reshape(t, d)
    for l in range(DEPTH):
        proj = _inproj(x2, attn_norm[l].reshape(1, d), _prep_w_in(w_in[l]))
        proj3 = proj.reshape(b, s, PROJ_W)
        wqn, wqr = _prep_w_uq(w_uq[l])
        wk, wvt = _prep_w_ukv(w_ukv[l])
        q3, k3, vt5 = _mla_up(proj3, q_norm[l].reshape(1, Q_LORA), kv_norm[l].reshape(1, KV_LORA),
                              wqn, wqr, wk, wvt, c64, s64)
        a3 = _attention(q3, k3, vt5)
        r3 = _retention(proj3, c128, s128, dec, xi, zeta, cd, beta_ret[l].reshape(1, -1))
        x2 = _outproj(a3.reshape(t, -1), r3.reshape(t, -1), x2, beta_attn[l].reshape(1, -1),
                      w_o[l].astype(BF16))
        x2 = _mlp(x2, mlp_norm[l].reshape(1, d), w_up[l].astype(BF16), w_down[l].astype(BF16))
    return _final_norm(x2, final_norm.reshape(1, d)).reshape(b, s, d)
```

```python
import functools

import jax
import jax.numpy as jnp
from jax import lax
from jax.experimental import pallas as pl
from jax.experimental.pallas import tpu as pltpu

F32 = jnp.float32
BF16 = jnp.bfloat16

D_MODEL = 2048
DEPTH = 4
MLA_HEADS = 8
NOPE = 128
ROPE = 64
VDIM = 128
Q_LORA = 512
KV_LORA = 256
RET_HEADS = 8
RET_DK = 128
RET_DV = 128
D_FF = 4 * D_MODEL
ROPE_BASE = 10000.0
NORM_EPS = 1e-6
GN_EPS = 1e-6
LOG2E = 1.4426950408889634

LANE = 128
QK_PAD = 256
MXU_N = 256
MLA_IN = 1024
PROJ_W = MLA_IN + 4 * RET_HEADS * RET_DK
RET_COL0 = MLA_IN // LANE

VMEM_LIMIT = 56 * 1024 * 1024

TM_PROJ = 1024
TN_PROJ = 5 * MXU_N
TK = 512
TM_MLA = TK
TQ = 1024
TQ_SUB = 256
TM_RET = 512
RET_GROUP = 8
TM_OUT = 512
TM_MLP = 512
TF_MLP = 1024
TM_TAB = 2048


def _cparams(sem):
    return pltpu.CompilerParams(dimension_semantics=sem, vmem_limit_bytes=VMEM_LIMIT)


def _rms(x, g):
    ms = jnp.mean(x * x, axis=-1, keepdims=True)
    return x * lax.rsqrt(ms + NORM_EPS) * g


def _rope_table_kernel(pos_ref, f128_ref, s128_ref, f64_ref, s64_ref,
                       c128_ref, sn128_ref, c64_ref, sn64_ref):
    pos = pos_ref[...].astype(F32)
    a128 = pos * f128_ref[...]
    c128_ref[...] = jnp.cos(a128)
    sn128_ref[...] = jnp.sin(a128) * s128_ref[...]
    a64 = pos * f64_ref[...]
    c64_ref[...] = jnp.cos(a64)
    sn64_ref[...] = jnp.sin(a64) * s64_ref[...]


def _rope_tables(positions):
    t = positions.size
    pos = positions.reshape(t, 1)
    inv128 = ROPE_BASE ** (-jnp.arange(0, RET_DK, 2, dtype=F32) / RET_DK)
    inv64 = ROPE_BASE ** (-jnp.arange(0, ROPE, 2, dtype=F32) / ROPE)
    f128 = jnp.concatenate([inv128, inv128]).reshape(1, LANE)
    s128 = jnp.concatenate([-jnp.ones(64, F32), jnp.ones(64, F32)]).reshape(1, LANE)
    f64 = jnp.tile(jnp.concatenate([inv64, inv64]), 2).reshape(1, LANE)
    s64 = jnp.tile(jnp.concatenate([-jnp.ones(32, F32), jnp.ones(32, F32)]), 2).reshape(1, LANE)
    row = pl.BlockSpec((1, LANE), lambda i: (0, 0))
    tab = pl.BlockSpec((TM_TAB, LANE), lambda i: (i, 0))
    shp = jax.ShapeDtypeStruct((t, LANE), F32)
    return pl.pallas_call(
        _rope_table_kernel,
        grid=(t // TM_TAB,),
        in_specs=[pl.BlockSpec((TM_TAB, 1), lambda i: (i, 0)), row, row, row, row],
        out_specs=[tab, tab, tab, tab],
        out_shape=[shp, shp, shp, shp],
        compiler_params=_cparams(("parallel",)),
        name="rope_tables",
    )(pos, f128, s128, f64, s64)


def _inproj_kernel(x_ref, g_ref, w_ref, o_ref, h_ref):
    @pl.when(pl.program_id(1) == 0)
    def _():
        h_ref[...] = _rms(x_ref[...], g_ref[...]).astype(BF16)

    o_ref[...] = jnp.dot(h_ref[...], w_ref[...], preferred_element_type=F32).astype(o_ref.dtype)


def _inproj(x2, g, w, l):
    t = x2.shape[0]
    return pl.pallas_call(
        _inproj_kernel,
        grid=(t // TM_PROJ, PROJ_W // TN_PROJ),
        in_specs=[
            pl.BlockSpec((TM_PROJ, D_MODEL), lambda i, j: (i, 0)),
            pl.BlockSpec((None, 1, D_MODEL), lambda i, j: (l, 0, 0)),
            pl.BlockSpec((None, D_MODEL, TN_PROJ), lambda i, j: (l, 0, j)),
        ],
        out_specs=pl.BlockSpec((TM_PROJ, TN_PROJ), lambda i, j: (i, j)),
        out_shape=jax.ShapeDtypeStruct((t, PROJ_W), BF16),
        scratch_shapes=[pltpu.VMEM((TM_PROJ, D_MODEL), BF16)],
        compiler_params=_cparams(("parallel", "arbitrary")),
        name="in_proj",
    )(x2, g, w)


def _swap32(x):
    lane = lax.broadcasted_iota(jnp.int32, x.shape, 1)
    return jnp.where((lane & 32) == 0, pltpu.roll(x, 96, 1), pltpu.roll(x, 32, 1))


def _mla_up_kernel(cq_ref, ckv_ref, kr_ref, qg_ref, kvg_ref, wqn_ref, wqr_ref, wk_ref, wvt_ref,
                   c64_ref, s64_ref, q_ref, k_ref, vt_ref):
    scale = (NOPE + ROPE) ** -0.5 * LOG2E
    cqn = _rms(cq_ref[...].astype(F32), qg_ref[...]).astype(BF16)
    ckvn = _rms(ckv_ref[...].astype(F32), kvg_ref[...]).astype(BF16)
    cos = c64_ref[...]
    sin = s64_ref[...]
    lane = lax.broadcasted_iota(jnp.int32, cos.shape, 1)
    low = lane < ROPE

    qn = jnp.dot(cqn, wqn_ref[...], preferred_element_type=F32)
    qr = jnp.dot(cqn, wqr_ref[...], preferred_element_type=F32)
    kn = jnp.dot(ckvn, wk_ref[...], preferred_element_type=F32)
    vt = lax.dot_general(wvt_ref[...], ckvn, (((1,), (1,)), ((), ())),
                         preferred_element_type=F32)

    kr = kr_ref[...].astype(F32)
    krr = (kr * cos + _swap32(kr) * sin).astype(BF16)

    for h in range(MLA_HEADS):
        q_ref[:, h * QK_PAD:h * QK_PAD + NOPE] = (qn[:, h * NOPE:(h + 1) * NOPE] * scale).astype(BF16)
        k_ref[:, h * QK_PAD:h * QK_PAD + NOPE] = kn[:, h * NOPE:(h + 1) * NOPE].astype(BF16)
        k_ref[:, h * QK_PAD + NOPE:(h + 1) * QK_PAD] = krr
        vt_ref[h] = vt[h * VDIM:(h + 1) * VDIM, :].astype(BF16)
    for p in range(MLA_HEADS // 2):
        x = qr[:, p * LANE:(p + 1) * LANE]
        xr = (x * cos + _swap32(x) * sin) * scale
        even = jnp.where(low, xr, 0.0)
        odd = jnp.where(low, pltpu.roll(xr, ROPE, 1), 0.0)
        q_ref[:, (2 * p) * QK_PAD + NOPE:(2 * p + 1) * QK_PAD] = even.astype(BF16)
        q_ref[:, (2 * p + 1) * QK_PAD + NOPE:(2 * p + 2) * QK_PAD] = odd.astype(BF16)


def _mla_up(proj3, qg, kvg, wqn, wqr, wk, wvt, c64, s64, l):
    b, s, _ = proj3.shape
    nc = s // TK
    full = lambda shape: pl.BlockSpec((None,) + shape, lambda bi, i: (l,) + (0,) * len(shape))
    return pl.pallas_call(
        _mla_up_kernel,
        grid=(b, nc),
        in_specs=[
            pl.BlockSpec((None, TM_MLA, Q_LORA), lambda bi, i: (bi, i, 0)),
            pl.BlockSpec((None, TM_MLA, KV_LORA), lambda bi, i: (bi, i, Q_LORA // KV_LORA)),
            pl.BlockSpec((None, TM_MLA, LANE), lambda bi, i: (bi, i, (Q_LORA + KV_LORA) // LANE)),
            full((1, Q_LORA)), full((1, KV_LORA)),
            full((Q_LORA, MLA_HEADS * NOPE)), full((Q_LORA, MLA_HEADS * ROPE)),
            full((KV_LORA, MLA_HEADS * NOPE)), full((MLA_HEADS * VDIM, KV_LORA)),
            pl.BlockSpec((None, TM_MLA, LANE), lambda bi, i: (bi, i, 0)),
            pl.BlockSpec((None, TM_MLA, LANE), lambda bi, i: (bi, i, 0)),
        ],
        out_specs=[
            pl.BlockSpec((None, TM_MLA, MLA_HEADS * QK_PAD), lambda bi, i: (bi, i, 0)),
            pl.BlockSpec((None, TM_MLA, MLA_HEADS * QK_PAD), lambda bi, i: (bi, i, 0)),
            pl.BlockSpec((None, MLA_HEADS, None, VDIM, TK), lambda bi, i: (bi, 0, i, 0, 0)),
        ],
        out_shape=[
            jax.ShapeDtypeStruct((b, s, MLA_HEADS * QK_PAD), BF16),
            jax.ShapeDtypeStruct((b, s, MLA_HEADS * QK_PAD), BF16),
            jax.ShapeDtypeStruct((b, MLA_HEADS, nc, VDIM, TK), BF16),
        ],
        compiler_params=_cparams(("parallel", "parallel")),
        name="mla_up",
    )(proj3, proj3, proj3, qg, kvg, wqn, wqr, wk, wvt, c64, s64)


def _attn_kernel(q_ref, k_ref, vt_ref, o_ref, qt_ref, s_ref, m_ref, l_ref, acc_ref):
    i = pl.program_id(2)
    qt_ref[...] = q_ref[...].astype(F32).T.astype(BF16)
    m_ref[...] = jnp.full(m_ref.shape, -jnp.inf, F32)
    l_ref[...] = jnp.zeros(l_ref.shape, F32)
    acc_ref[...] = jnp.zeros(acc_ref.shape, F32)

    def scores(j, slot, q_lo=0):
        kc = k_ref[pl.ds(pl.multiple_of(j * TK, TK), TK), :]
        s_ref[slot, :, q_lo:TQ] = jnp.dot(kc, qt_ref[:, q_lo:], preferred_element_type=F32)

    def update(j, slot, diag):
        for c in range(TQ // TQ_SUB):
            q_lo = c * TQ_SUB
            cols = slice(q_lo, q_lo + TQ_SUB)
            nk, masked, k_lo = TK, False, 0
            if diag is not None:
                k_lo = diag * TK
                if k_lo >= q_lo + TQ_SUB:
                    continue
                nk = min(TK, q_lo + TQ_SUB - k_lo)
                masked = k_lo + nk - 1 > q_lo
            st = s_ref[slot, :nk, cols]
            if masked:
                row = lax.broadcasted_iota(jnp.int32, st.shape, 0) + k_lo
                col = lax.broadcasted_iota(jnp.int32, st.shape, 1) + q_lo
                st = jnp.where(row <= col, st, -jnp.inf)
            m_old = m_ref[:, cols]
            m_new = jnp.maximum(m_old, jnp.max(st, axis=0, keepdims=True))
            alpha = jnp.exp2(m_old - m_new)
            p = jnp.exp2(st - m_new)
            l_ref[:, cols] = alpha * l_ref[:, cols] + jnp.sum(p, axis=0, keepdims=True)
            pv = jnp.dot(vt_ref[j][:, :nk], p.astype(BF16), preferred_element_type=F32)
            acc_ref[:, cols] = alpha * acc_ref[:, cols] + pv
            m_ref[:, cols] = m_new

    band = TQ // TK
    first_diag = band * i
    scores(0, 0)

    def body(t, carry):
        j = 2 * t
        scores(j + 1, 1)
        update(j, 0, None)
        scores(j + 2, 0)
        update(j + 1, 1, None)
        return carry

    lax.fori_loop(0, i * (band // 2), body, 0)

    for d in range(band):
        if d + 1 < band:
            scores(first_diag + d + 1, (d + 1) % 2, (d + 1) * TK)
        update(first_diag + d, d % 2, d)

    o_ref[...] = (acc_ref[...] / l_ref[...]).T


def _attention(q3, k3, vt5):
    b, s, _ = q3.shape
    nq = s // TQ
    return pl.pallas_call(
        _attn_kernel,
        grid=(b, MLA_HEADS, nq),
        in_specs=[
            pl.BlockSpec((None, TQ, QK_PAD), lambda bi, h, i: (bi, i, h)),
            pl.BlockSpec((None, s, QK_PAD), lambda bi, h, i: (bi, 0, h)),
            pl.BlockSpec((None, None, s // TK, VDIM, TK), lambda bi, h, i: (bi, h, 0, 0, 0)),
        ],
        out_specs=pl.BlockSpec((None, TQ, VDIM), lambda bi, h, i: (bi, i, h)),
        out_shape=jax.ShapeDtypeStruct((b, s, MLA_HEADS * VDIM), F32),
        scratch_shapes=[
            pltpu.VMEM((QK_PAD, TQ), BF16),
            pltpu.VMEM((2, TK, TQ + LANE), F32),
            pltpu.VMEM((1, TQ), F32),
            pltpu.VMEM((1, TQ), F32),
            pltpu.VMEM((VDIM, TQ), F32),
        ],
        compiler_params=_cparams(("parallel", "parallel", "arbitrary")),
        name="mla_attention",
    )(q3, k3, vt5)


def _ret_kernel(rq_ref, rk_ref, rv_ref, rg_ref, c_ref, s_ref, dec_ref, xi_ref, zeta_ref, cd_ref, beta_ref,
                o_ref, state_ref):
    @pl.when(pl.program_id(2) == 0)
    def _():
        state_ref[...] = jnp.zeros(state_ref.shape, F32)

    cos = c_ref[...]
    sin = s_ref[...]
    for g in range(RET_GROUP):
        cols = slice(g * RET_DK, (g + 1) * RET_DK)
        rq = rq_ref[:, cols].astype(F32)
        rk = rk_ref[:, cols].astype(F32)
        v = rv_ref[:, cols]
        q = (rq * cos + pltpu.roll(rq, RET_DK // 2, 1) * sin) * (RET_DK ** -0.5)
        k = rk * cos + pltpu.roll(rk, RET_DK // 2, 1) * sin
        scores = lax.dot_general(q.astype(BF16), k.astype(BF16), (((1,), (1,)), ((), ())),
                                 preferred_element_type=F32) * dec_ref[g]
        st = state_ref[g]
        o = (jnp.dot(scores.astype(BF16), v, preferred_element_type=F32)
             + jnp.dot((q * xi_ref[g]).astype(BF16), st.astype(BF16), preferred_element_type=F32))
        kz = (k * zeta_ref[g]).astype(BF16)
        state_ref[g] = cd_ref[g] * st + lax.dot_general(kz, v, (((0,), (0,)), ((), ())),
                                                        preferred_element_type=F32)
        mu = jnp.mean(o, axis=-1, keepdims=True)
        d = o - mu
        var = jnp.mean(d * d, axis=-1, keepdims=True)
        on = d * lax.rsqrt(var + GN_EPS)
        gate = rg_ref[:, cols].astype(F32)
        silu = gate * (1.0 / (1.0 + jnp.exp(-gate)))
        o_ref[:, cols] = (on * silu * beta_ref[:, cols]).astype(o_ref.dtype)


def _retention(proj3, c128, s128, dec, xi, zeta, cd, beta, l):
    b, s, _ = proj3.shape
    gw = RET_GROUP * RET_DK
    ngroups = RET_HEADS // RET_GROUP
    seg = lambda k: pl.BlockSpec((None, TM_RET, gw),
                                 lambda bi, h, i: (bi, i, (RET_COL0 + k * RET_HEADS) // RET_GROUP + h))
    tab = pl.BlockSpec((None, TM_RET, LANE), lambda bi, h, i: (bi, i, 0))
    per_head = lambda rows, width: pl.BlockSpec((RET_GROUP, rows, width), lambda bi, h, i: (h, 0, 0))
    return pl.pallas_call(
        _ret_kernel,
        grid=(b, ngroups, s // TM_RET),
        in_specs=[seg(0), seg(1), seg(2), seg(3), tab, tab,
                  per_head(TM_RET, TM_RET), per_head(TM_RET, LANE), per_head(TM_RET, LANE), per_head(RET_DK, LANE),
                  pl.BlockSpec((None, 1, gw), lambda bi, h, i: (l, 0, h))],
        out_specs=pl.BlockSpec((None, TM_RET, gw), lambda bi, h, i: (bi, i, h)),
        out_shape=jax.ShapeDtypeStruct((b, s, RET_HEADS * RET_DV), BF16),
        scratch_shapes=[pltpu.VMEM((RET_GROUP, RET_DK, RET_DV), F32)],
        compiler_params=_cparams(("parallel", "parallel", "arbitrary")),
        name="retention",
    )(proj3, proj3, proj3, proj3, c128, s128, dec, xi, zeta, cd, beta)


def _retention_constants():
    h, c = RET_HEADS, TM_RET
    log_gamma = jnp.log1p(-jnp.exp2(-5.0 - jnp.arange(h, dtype=F32)))
    idx = jnp.arange(c, dtype=F32)
    rel = idx[:, None] - idx[None, :]
    dec = jnp.where(rel >= 0, jnp.exp(log_gamma[:, None, None] * jnp.maximum(rel, 0.0)), 0.0)
    xi = jnp.exp(log_gamma[:, None] * (idx + 1.0))
    zeta = jnp.exp(log_gamma[:, None] * (c - 1.0 - idx))
    cd = jnp.exp(log_gamma * c)
    bc = lambda t: jnp.broadcast_to(t[:, :, None], (h, c, LANE))
    return dec, bc(xi), bc(zeta), jnp.broadcast_to(cd[:, None, None], (h, RET_DK, LANE))


def _outproj_kernel(a_ref, r_ref, x_ref, ba_ref, wt_ref, wb_ref, o_ref):
    an = _rms(a_ref[...], ba_ref[...]).astype(BF16)
    y = jnp.dot(an, wt_ref[...], preferred_element_type=F32)
    y = y + jnp.dot(r_ref[...], wb_ref[...], preferred_element_type=F32)
    o_ref[...] = x_ref[...] + y


def _outproj(a2, r2, x2, ba, wo, l):
    t = x2.shape[0]
    half = MLA_HEADS * VDIM
    return pl.pallas_call(
        _outproj_kernel,
        grid=(t // TM_OUT,),
        in_specs=[
            pl.BlockSpec((TM_OUT, half), lambda i: (i, 0)),
            pl.BlockSpec((TM_OUT, half), lambda i: (i, 0)),
            pl.BlockSpec((TM_OUT, D_MODEL), lambda i: (i, 0)),
            pl.BlockSpec((None, 1, half), lambda i: (l, 0, 0)),
            pl.BlockSpec((None, half, D_MODEL), lambda i: (l, 0, 0)),
            pl.BlockSpec((None, half, D_MODEL), lambda i: (l, 1, 0)),
        ],
        out_specs=pl.BlockSpec((TM_OUT, D_MODEL), lambda i: (i, 0)),
        out_shape=jax.ShapeDtypeStruct((t, D_MODEL), F32),
        compiler_params=_cparams(("parallel",)),
        name="out_proj",
    )(a2, r2, x2, ba, wo, wo)


def _mlp_kernel(x_ref, g_ref, wu_ref, wd_ref, fg_ref, o_ref, h_ref, *, final):
    j = pl.program_id(1)

    @pl.when(j == 0)
    def _():
        x = x_ref[...]
        h_ref[...] = _rms(x, g_ref[...]).astype(BF16)
        o_ref[...] = x

    u = jnp.maximum(jnp.dot(h_ref[...], wu_ref[...], preferred_element_type=F32), 0.0)
    u = (u * u).astype(BF16)
    o_ref[...] += jnp.dot(u, wd_ref[...], preferred_element_type=F32)

    if final:
        @pl.when(j == pl.num_programs(1) - 1)
        def _():
            o_ref[...] = _rms(o_ref[...], fg_ref[...])


def _mlp(x2, g, wu, wd, fg, l, final):
    t = x2.shape[0]
    return pl.pallas_call(
        functools.partial(_mlp_kernel, final=final),
        grid=(t // TM_MLP, D_FF // TF_MLP),
        in_specs=[
            pl.BlockSpec((TM_MLP, D_MODEL), lambda i, j: (i, 0)),
            pl.BlockSpec((None, 1, D_MODEL), lambda i, j: (l, 0, 0)),
            pl.BlockSpec((None, D_MODEL, TF_MLP), lambda i, j: (l, 0, j)),
            pl.BlockSpec((None, TF_MLP, D_MODEL), lambda i, j: (l, j, 0)),
            pl.BlockSpec((1, D_MODEL), lambda i, j: (0, 0)),
        ],
        out_specs=pl.BlockSpec((TM_MLP, D_MODEL), lambda i, j: (i, 0)),
        out_shape=jax.ShapeDtypeStruct((t, D_MODEL), F32),
        scratch_shapes=[pltpu.VMEM((TM_MLP, D_MODEL), BF16)],
        compiler_params=_cparams(("parallel", "arbitrary")),
        name="mlp",
    )(x2, g, wu, wd, fg)


def _prep_w_in(w):
    cut = Q_LORA + KV_LORA + ROPE
    pad = jnp.zeros(w.shape[:2] + (MLA_IN - cut,), w.dtype)
    return jnp.concatenate([w[..., :cut], pad, w[..., cut:]], axis=-1).astype(BF16)


def _prep_w_uq(w):
    w4 = w.reshape(DEPTH, Q_LORA, MLA_HEADS, NOPE + ROPE)
    wqn = w4[..., :NOPE].reshape(DEPTH, Q_LORA, MLA_HEADS * NOPE).astype(BF16)
    wqr = w4[..., NOPE:].reshape(DEPTH, Q_LORA, MLA_HEADS * ROPE).astype(BF16)
    return wqn, wqr


def _prep_w_ukv(w):
    w4 = w.reshape(DEPTH, KV_LORA, MLA_HEADS, NOPE + VDIM)
    wk = w4[..., :NOPE].reshape(DEPTH, KV_LORA, MLA_HEADS * NOPE).astype(BF16)
    wvt = jnp.swapaxes(w4[..., NOPE:].reshape(DEPTH, KV_LORA, MLA_HEADS * VDIM), 1, 2).astype(BF16)
    return wk, wvt


def kernel(x, positions, attn_norm, w_in, q_norm, kv_norm, w_uq, w_ukv, beta_attn, beta_ret,
           w_o, mlp_norm, w_up, w_down, final_norm):
    b, s, d = x.shape
    t = b * s
    c128, s128, c64, s64 = _rope_tables(positions)
    c128, s128, c64, s64 = (a.reshape(b, s, LANE) for a in (c128, s128, c64, s64))
    dec, xi, zeta, cd = _retention_constants()

    row = lambda g: g.reshape(DEPTH, 1, -1)
    w_in_b = _prep_w_in(w_in)
    wqn, wqr = _prep_w_uq(w_uq)
    wk, wvt = _prep_w_ukv(w_ukv)
    w_o_b, w_up_b, w_down_b = w_o.astype(BF16), w_up.astype(BF16), w_down.astype(BF16)
    fg = final_norm.reshape(1, d)

    x2 = x.reshape(t, d)
    for l in range(DEPTH):
        proj3 = _inproj(x2, row(attn_norm), w_in_b, l).reshape(b, s, PROJ_W)
        q3, k3, vt5 = _mla_up(proj3, row(q_norm), row(kv_norm), wqn, wqr, wk, wvt, c64, s64, l)
        a3 = _attention(q3, k3, vt5)
        r3 = _retention(proj3, c128, s128, dec, xi, zeta, cd, row(beta_ret), l)
        x2 = _outproj(a3.reshape(t, -1), r3.reshape(t, -1), x2, row(beta_attn), w_o_b, l)
        x2 = _mlp(x2, row(mlp_norm), w_up_b, w_down_b, fg, l, final=(l == DEPTH - 1))
    return x2.reshape(b, s, d)
```

```python
import functools

import jax
import jax.numpy as jnp
from jax import lax
from jax.experimental import pallas as pl
from jax.experimental.pallas import tpu as pltpu

F32 = jnp.float32
BF16 = jnp.bfloat16

D_MODEL = 2048
DEPTH = 4
MLA_HEADS = 8
NOPE = 128
ROPE = 64
VDIM = 128
Q_LORA = 512
KV_LORA = 256
RET_HEADS = 8
RET_DK = 128
RET_DV = 128
D_FF = 4 * D_MODEL
ROPE_BASE = 10000.0
NORM_EPS = 1e-6
GN_EPS = 1e-6
LOG2E = 1.4426950408889634

LANE = 128
QK_PAD = 256
V_AUG = VDIM + 16
MXU_N = 256
MLA_IN = 1024
PROJ_W = MLA_IN + 4 * RET_HEADS * RET_DK
RET_COL0 = MLA_IN // LANE

VMEM_LIMIT = 56 * 1024 * 1024

TM_PROJ = 1024
TN_PROJ = MLA_IN
TK = 512
TM_MLA = TK
TQ = 1024
TQ_SUB = 256
TM_RET = 512
RET_GROUP = 8
TM_OUT = 512
TM_MLP = 512
TF_MLP = 1024
TM_TAB = 2048
TM_WSPLIT = 256


def _cparams(sem, flags=None):
    return pltpu.CompilerParams(dimension_semantics=sem, vmem_limit_bytes=VMEM_LIMIT, flags=flags)


def _rms(x, g):
    ms = jnp.mean(x * x, axis=-1, keepdims=True)
    return x * lax.rsqrt(ms + NORM_EPS) * g


def _rope_table_kernel(pos_ref, f128_ref, s128_ref, f64_ref, s64_ref,
                       c128_ref, sn128_ref, c64_ref, sn64_ref):
    pos = pos_ref[...].astype(F32)
    a128 = pos * f128_ref[...]
    c128_ref[...] = jnp.cos(a128)
    sn128_ref[...] = jnp.sin(a128) * s128_ref[...]
    a64 = pos * f64_ref[...]
    c64_ref[...] = jnp.cos(a64)
    sn64_ref[...] = jnp.sin(a64) * s64_ref[...]


def _rope_tables(positions):
    t = positions.size
    pos = positions.reshape(t, 1)
    inv128 = ROPE_BASE ** (-jnp.arange(0, RET_DK, 2, dtype=F32) / RET_DK)
    inv64 = ROPE_BASE ** (-jnp.arange(0, ROPE, 2, dtype=F32) / ROPE)
    f128 = jnp.concatenate([inv128, inv128]).reshape(1, LANE)
    s128 = jnp.concatenate([-jnp.ones(64, F32), jnp.ones(64, F32)]).reshape(1, LANE)
    f64 = jnp.tile(jnp.concatenate([inv64, inv64]), 2).reshape(1, LANE)
    s64 = jnp.tile(jnp.concatenate([-jnp.ones(32, F32), jnp.ones(32, F32)]), 2).reshape(1, LANE)
    row = pl.BlockSpec((1, LANE), lambda i: (0, 0))
    tab = pl.BlockSpec((TM_TAB, LANE), lambda i: (i, 0))
    shp = jax.ShapeDtypeStruct((t, LANE), F32)
    return pl.pallas_call(
        _rope_table_kernel,
        grid=(t // TM_TAB,),
        in_specs=[pl.BlockSpec((TM_TAB, 1), lambda i: (i, 0)), row, row, row, row],
        out_specs=[tab, tab, tab, tab],
        out_shape=[shp, shp, shp, shp],
        compiler_params=_cparams(("parallel",)),
        name="rope_tables",
    )(pos, f128, s128, f64, s64)


def _inproj_kernel(x_ref, g_ref, wm_ref, wr_ref, o_ref, h_ref):
    j = pl.program_id(1)

    @pl.when(j == 0)
    def _():
        h = _rms(x_ref[...], g_ref[...]).astype(BF16)
        h_ref[...] = h
        o_ref[...] = jnp.dot(h, wm_ref[...], preferred_element_type=F32).astype(o_ref.dtype)

    @pl.when(j > 0)
    def _():
        o_ref[...] = jnp.dot(h_ref[...], wr_ref[...], preferred_element_type=F32).astype(o_ref.dtype)


def _inproj(x2, g, wm, wr, l):
    t = x2.shape[0]
    return pl.pallas_call(
        _inproj_kernel,
        grid=(t // TM_PROJ, PROJ_W // TN_PROJ),
        in_specs=[
            pl.BlockSpec((TM_PROJ, D_MODEL), lambda i, j: (i, 0)),
            pl.BlockSpec((None, 1, D_MODEL), lambda i, j: (l, 0, 0)),
            pl.BlockSpec((None, D_MODEL, MLA_IN), lambda i, j: (l, 0, 0)),
            pl.BlockSpec((None, D_MODEL, TN_PROJ), lambda i, j: (l, 0, jnp.maximum(j - 1, 0))),
        ],
        out_specs=pl.BlockSpec((TM_PROJ, TN_PROJ), lambda i, j: (i, j)),
        out_shape=jax.ShapeDtypeStruct((t, PROJ_W), BF16),
        scratch_shapes=[pltpu.VMEM((TM_PROJ, D_MODEL), BF16)],
        compiler_params=_cparams(("parallel", "arbitrary")),
        name="in_proj",
    )(x2, g, wm, wr)


def _swap32(x):
    lane = lax.broadcasted_iota(jnp.int32, x.shape, 1)
    return jnp.where((lane & 32) == 0, pltpu.roll(x, 96, 1), pltpu.roll(x, 32, 1))


def _mla_up_kernel(cq_ref, ckv_ref, kr_ref, qg_ref, kvg_ref, wqn_ref, wqr_ref, wk_ref, wvt_ref,
                   c64_ref, s64_ref, q_ref, k_ref, vt_ref):
    scale = (NOPE + ROPE) ** -0.5 * LOG2E
    cqn = _rms(cq_ref[...].astype(F32), qg_ref[...]).astype(BF16)
    ckvn = _rms(ckv_ref[...].astype(F32), kvg_ref[...]).astype(BF16)
    cos = c64_ref[...]
    sin = s64_ref[...]
    lane = lax.broadcasted_iota(jnp.int32, cos.shape, 1)
    low = lane < ROPE

    qn = jnp.dot(cqn, wqn_ref[...], preferred_element_type=F32)
    qr = jnp.dot(cqn, wqr_ref[...], preferred_element_type=F32)
    kn = jnp.dot(ckvn, wk_ref[...], preferred_element_type=F32)
    vt = lax.dot_general(wvt_ref[...], ckvn, (((1,), (1,)), ((), ())),
                         preferred_element_type=F32)

    kr = kr_ref[...].astype(F32)
    krr = (kr * cos + _swap32(kr) * sin).astype(BF16)

    for h in range(MLA_HEADS):
        q_ref[:, h * QK_PAD:h * QK_PAD + NOPE] = (qn[:, h * NOPE:(h + 1) * NOPE] * scale).astype(BF16)
        k_ref[:, h * QK_PAD:h * QK_PAD + NOPE] = kn[:, h * NOPE:(h + 1) * NOPE].astype(BF16)
        k_ref[:, h * QK_PAD + NOPE:(h + 1) * QK_PAD] = krr
        vt_ref[h, :VDIM, :] = vt[h * VDIM:(h + 1) * VDIM, :].astype(BF16)
        vt_ref[h, VDIM:, :] = jnp.ones((V_AUG - VDIM, vt.shape[1]), BF16)
    for p in range(MLA_HEADS // 2):
        x = qr[:, p * LANE:(p + 1) * LANE]
        xr = (x * cos + _swap32(x) * sin) * scale
        even = jnp.where(low, xr, 0.0)
        odd = jnp.where(low, pltpu.roll(xr, ROPE, 1), 0.0)
        q_ref[:, (2 * p) * QK_PAD + NOPE:(2 * p + 1) * QK_PAD] = even.astype(BF16)
        q_ref[:, (2 * p + 1) * QK_PAD + NOPE:(2 * p + 2) * QK_PAD] = odd.astype(BF16)


def _mla_up(proj3, qg, kvg, wqn, wqr, wk, wvt, c64, s64, l):
    b, s, _ = proj3.shape
    nc = s // TK
    full = lambda shape: pl.BlockSpec((None,) + shape, lambda bi, i: (l,) + (0,) * len(shape))
    return pl.pallas_call(
        _mla_up_kernel,
        grid=(b, nc),
        in_specs=[
            pl.BlockSpec((None, TM_MLA, Q_LORA), lambda bi, i: (bi, i, 0)),
            pl.BlockSpec((None, TM_MLA, KV_LORA), lambda bi, i: (bi, i, Q_LORA // KV_LORA)),
            pl.BlockSpec((None, TM_MLA, LANE), lambda bi, i: (bi, i, (Q_LORA + KV_LORA) // LANE)),
            full((1, Q_LORA)), full((1, KV_LORA)),
            full((Q_LORA, MLA_HEADS * NOPE)), full((Q_LORA, MLA_HEADS * ROPE)),
            full((KV_LORA, MLA_HEADS * NOPE)), full((MLA_HEADS * VDIM, KV_LORA)),
            pl.BlockSpec((None, TM_MLA, LANE), lambda bi, i: (bi, i, 0)),
            pl.BlockSpec((None, TM_MLA, LANE), lambda bi, i: (bi, i, 0)),
        ],
        out_specs=[
            pl.BlockSpec((None, TM_MLA, MLA_HEADS * QK_PAD), lambda bi, i: (bi, i, 0)),
            pl.BlockSpec((None, TM_MLA, MLA_HEADS * QK_PAD), lambda bi, i: (bi, i, 0)),
            pl.BlockSpec((None, MLA_HEADS, None, V_AUG, TK), lambda bi, i: (bi, 0, i, 0, 0)),
        ],
        out_shape=[
            jax.ShapeDtypeStruct((b, s, MLA_HEADS * QK_PAD), BF16),
            jax.ShapeDtypeStruct((b, s, MLA_HEADS * QK_PAD), BF16),
            jax.ShapeDtypeStruct((b, MLA_HEADS, nc, V_AUG, TK), BF16),
        ],
        compiler_params=_cparams(("parallel", "parallel")),
        name="mla_up",
    )(proj3, proj3, proj3, qg, kvg, wqn, wqr, wk, wvt, c64, s64)


def _attn_kernel(q_ref, k_ref, vt_ref, o_ref, qt_ref, s_ref, cmax_ref, m_ref, acc_ref):
    i = pl.program_id(2)
    qt_ref[...] = q_ref[...].astype(F32).T.astype(BF16)
    m_ref[...] = jnp.full(m_ref.shape, -jnp.inf, F32)
    acc_ref[...] = jnp.zeros(acc_ref.shape, F32)

    def scores(j, slot, q_lo=0):
        kc = k_ref[pl.ds(pl.multiple_of(j * TK, TK), TK), :]
        st = jnp.dot(kc, qt_ref[:, q_lo:], preferred_element_type=F32)
        s_ref[slot, :, q_lo:TQ] = st
        if q_lo == 0:
            cmax_ref[slot] = jnp.max(st, axis=0, keepdims=True)

    def update(j, slot, diag):
        for c in range(TQ // TQ_SUB):
            q_lo = c * TQ_SUB
            cols = slice(q_lo, q_lo + TQ_SUB)
            nk, masked, k_lo = TK, False, 0
            if diag is not None:
                k_lo = diag * TK
                if k_lo >= q_lo + TQ_SUB:
                    continue
                nk = min(TK, q_lo + TQ_SUB - k_lo)
                masked = k_lo + nk - 1 > q_lo
            st = s_ref[slot, :nk, cols]
            if masked:
                row = lax.broadcasted_iota(jnp.int32, st.shape, 0) + k_lo
                col = lax.broadcasted_iota(jnp.int32, st.shape, 1) + q_lo
                st = jnp.where(row <= col, st, -jnp.inf)
            m_old = m_ref[:, cols]
            cmax = cmax_ref[slot][:, cols] if diag is None else jnp.max(st, axis=0, keepdims=True)
            m_new = jnp.maximum(m_old, cmax)
            alpha = jnp.exp2(m_old - m_new)
            p = jnp.exp2(st - m_new).astype(BF16)
            pv = jnp.dot(vt_ref[j][:, :nk], p, preferred_element_type=F32)
            acc_ref[:, cols] = alpha * acc_ref[:, cols] + pv
            m_ref[:, cols] = m_new

    band = TQ // TK
    first_diag = band * i
    scores(0, 0)

    def body(t, carry):
        j = 2 * t
        scores(j + 1, 1)
        update(j, 0, None)
        scores(j + 2, 0)
        update(j + 1, 1, None)
        return carry

    lax.fori_loop(0, i * (band // 2), body, 0)

    for d in range(band):
        if d + 1 < band:
            scores(first_diag + d + 1, (d + 1) % 2, (d + 1) * TK)
        update(first_diag + d, d % 2, d)

    o_ref[...] = (acc_ref[:VDIM, :] / acc_ref[VDIM:VDIM + 1, :]).T


def _attention(q3, k3, vt5):
    b, s, _ = q3.shape
    nq = s // TQ
    return pl.pallas_call(
        _attn_kernel,
        grid=(b, MLA_HEADS, nq),
        in_specs=[
            pl.BlockSpec((None, TQ, QK_PAD), lambda bi, h, i: (bi, i, h)),
            pl.BlockSpec((None, s, QK_PAD), lambda bi, h, i: (bi, 0, h)),
            pl.BlockSpec((None, None, s // TK, V_AUG, TK), lambda bi, h, i: (bi, h, 0, 0, 0)),
        ],
        out_specs=pl.BlockSpec((None, TQ, VDIM), lambda bi, h, i: (bi, i, h)),
        out_shape=jax.ShapeDtypeStruct((b, s, MLA_HEADS * VDIM), F32),
        scratch_shapes=[
            pltpu.VMEM((QK_PAD, TQ), BF16),
            pltpu.VMEM((2, TK, TQ + LANE), F32),
            pltpu.VMEM((2, 1, TQ), F32),
            pltpu.VMEM((1, TQ), F32),
            pltpu.VMEM((V_AUG, TQ), F32),
        ],
        compiler_params=_cparams(("parallel", "parallel", "arbitrary")),
        name="mla_attention",
    )(q3, k3, vt5)


def _ret_kernel(rq_ref, rk_ref, rv_ref, rg_ref, c_ref, s_ref, dec_ref, xi_ref, zeta_ref, cd_ref, beta_ref,
                o_ref, state_ref):
    @pl.when(pl.program_id(2) == 0)
    def _():
        state_ref[...] = jnp.zeros(state_ref.shape, F32)

    cos = c_ref[...]
    sin = s_ref[...]
    for g in range(RET_GROUP):
        cols = slice(g * RET_DK, (g + 1) * RET_DK)
        rq = rq_ref[:, cols].astype(F32)
        rk = rk_ref[:, cols].astype(F32)
        v = rv_ref[:, cols]
        q = (rq * cos + pltpu.roll(rq, RET_DK // 2, 1) * sin) * (RET_DK ** -0.5)
        k = rk * cos + pltpu.roll(rk, RET_DK // 2, 1) * sin
        scores = lax.dot_general(q.astype(BF16), k.astype(BF16), (((1,), (1,)), ((), ())),
                                 preferred_element_type=F32) * dec_ref[g]
        st = state_ref[g]
        o = (jnp.dot(scores.astype(BF16), v, preferred_element_type=F32)
             + jnp.dot((q * xi_ref[g]).astype(BF16), st.astype(BF16), preferred_element_type=F32))
        kz = (k * zeta_ref[g]).astype(BF16)
        state_ref[g] = cd_ref[g] * st + lax.dot_general(kz, v, (((0,), (0,)), ((), ())),
                                                        preferred_element_type=F32)
        mu = jnp.mean(o, axis=-1, keepdims=True)
        d = o - mu
        var = jnp.mean(d * d, axis=-1, keepdims=True)
        on = d * lax.rsqrt(var + GN_EPS)
        gate = rg_ref[:, cols].astype(F32)
        silu = gate * (1.0 / (1.0 + jnp.exp(-gate)))
        o_ref[:, cols] = (on * silu * beta_ref[:, cols]).astype(o_ref.dtype)


def _retention(proj3, c128, s128, dec, xi, zeta, cd, beta, l):
    b, s, _ = proj3.shape
    gw = RET_GROUP * RET_DK
    ngroups = RET_HEADS // RET_GROUP
    seg = lambda k: pl.BlockSpec((None, TM_RET, gw),
                                 lambda bi, h, i: (bi, i, (RET_COL0 + k * RET_HEADS) // RET_GROUP + h))
    tab = pl.BlockSpec((None, TM_RET, LANE), lambda bi, h, i: (bi, i, 0))
    per_head = lambda rows, width: pl.BlockSpec((RET_GROUP, rows, width), lambda bi, h, i: (h, 0, 0))
    return pl.pallas_call(
        _ret_kernel,
        grid=(b, ngroups, s // TM_RET),
        in_specs=[seg(0), seg(1), seg(2), seg(3), tab, tab,
                  per_head(TM_RET, TM_RET), per_head(TM_RET, LANE), per_head(TM_RET, LANE), per_head(RET_DK, LANE),
                  pl.BlockSpec((None, 1, gw), lambda bi, h, i: (l, 0, h))],
        out_specs=pl.BlockSpec((None, TM_RET, gw), lambda bi, h, i: (bi, i, h)),
        out_shape=jax.ShapeDtypeStruct((b, s, RET_HEADS * RET_DV), BF16),
        scratch_shapes=[pltpu.VMEM((RET_GROUP, RET_DK, RET_DV), F32)],
        compiler_params=_cparams(("parallel", "parallel", "arbitrary")),
        name="retention",
    )(proj3, proj3, proj3, proj3, c128, s128, dec, xi, zeta, cd, beta)


def _retention_constants():
    h, c = RET_HEADS, TM_RET
    log_gamma = jnp.log1p(-jnp.exp2(-5.0 - jnp.arange(h, dtype=F32)))
    idx = jnp.arange(c, dtype=F32)
    rel = idx[:, None] - idx[None, :]
    dec = jnp.where(rel >= 0, jnp.exp(log_gamma[:, None, None] * jnp.maximum(rel, 0.0)), 0.0)
    xi = jnp.exp(log_gamma[:, None] * (idx + 1.0))
    zeta = jnp.exp(log_gamma[:, None] * (c - 1.0 - idx))
    cd = jnp.exp(log_gamma * c)
    bc = lambda t: jnp.broadcast_to(t[:, :, None], (h, c, LANE))
    return dec, bc(xi), bc(zeta), jnp.broadcast_to(cd[:, None, None], (h, RET_DK, LANE))


def _outproj_kernel(a_ref, r_ref, x_ref, ba_ref, wt_ref, wb_ref, o_ref):
    an = _rms(a_ref[...], ba_ref[...]).astype(BF16)
    y = jnp.dot(an, wt_ref[...], preferred_element_type=F32)
    y = y + jnp.dot(r_ref[...], wb_ref[...], preferred_element_type=F32)
    o_ref[...] = x_ref[...] + y


def _outproj(a2, r2, x2, ba, wo, l):
    t = x2.shape[0]
    half = MLA_HEADS * VDIM
    return pl.pallas_call(
        _outproj_kernel,
        grid=(t // TM_OUT,),
        in_specs=[
            pl.BlockSpec((TM_OUT, half), lambda i: (i, 0)),
            pl.BlockSpec((TM_OUT, half), lambda i: (i, 0)),
            pl.BlockSpec((TM_OUT, D_MODEL), lambda i: (i, 0)),
            pl.BlockSpec((None, 1, half), lambda i: (l, 0, 0)),
            pl.BlockSpec((None, half, D_MODEL), lambda i: (l, 0, 0)),
            pl.BlockSpec((None, half, D_MODEL), lambda i: (l, 1, 0)),
        ],
        out_specs=pl.BlockSpec((TM_OUT, D_MODEL), lambda i: (i, 0)),
        out_shape=jax.ShapeDtypeStruct((t, D_MODEL), F32),
        compiler_params=_cparams(("parallel",)),
        name="out_proj",
    )(a2, r2, x2, ba, wo, wo)


def _mlp_kernel(x_ref, g_ref, wu_ref, wd_ref, fg_ref, o_ref, h_ref, *, final):
    j = pl.program_id(1)

    @pl.when(j == 0)
    def _():
        x = x_ref[...]
        h_ref[...] = _rms(x, g_ref[...]).astype(BF16)
        o_ref[...] = x

    u = jnp.maximum(jnp.dot(h_ref[...], wu_ref[...], preferred_element_type=F32), 0.0)
    u = (u * u).astype(BF16)
    o_ref[...] += jnp.dot(u, wd_ref[...], preferred_element_type=F32)

    if final:
        @pl.when(j == pl.num_programs(1) - 1)
        def _():
            o_ref[...] = _rms(o_ref[...], fg_ref[...])


def _mlp(x2, g, wu, wd, fg, l, final):
    t = x2.shape[0]
    return pl.pallas_call(
        functools.partial(_mlp_kernel, final=final),
        grid=(t // TM_MLP, D_FF // TF_MLP),
        in_specs=[
            pl.BlockSpec((TM_MLP, D_MODEL), lambda i, j: (i, 0)),
            pl.BlockSpec((None, 1, D_MODEL), lambda i, j: (l, 0, 0)),
            pl.BlockSpec((None, D_MODEL, TF_MLP), lambda i, j: (l, 0, j)),
            pl.BlockSpec((None, TF_MLP, D_MODEL), lambda i, j: (l, j, 0)),
            pl.BlockSpec((1, D_MODEL), lambda i, j: (0, 0)),
        ],
        out_specs=pl.BlockSpec((TM_MLP, D_MODEL), lambda i, j: (i, 0)),
        out_shape=jax.ShapeDtypeStruct((t, D_MODEL), F32),
        scratch_shapes=[pltpu.VMEM((TM_MLP, D_MODEL), BF16)],
        compiler_params=_cparams(("parallel", "arbitrary")),
        name="mlp",
    )(x2, g, wu, wd, fg)


def _split_w_in_kernel(w_ref, wm_ref, wr_ref):
    cut = Q_LORA + KV_LORA + ROPE
    w = w_ref[...]
    wm_ref[:, :cut] = w[:, :cut].astype(BF16)
    wm_ref[:, cut:] = jnp.zeros((w.shape[0], MLA_IN - cut), BF16)
    wr_ref[...] = w[:, cut:].astype(BF16)


def _prep_w_in(w):
    depth, d, n = w.shape
    n_ret = n - (Q_LORA + KV_LORA + ROPE)
    return pl.pallas_call(
        _split_w_in_kernel,
        grid=(depth, d // TM_WSPLIT),
        in_specs=[pl.BlockSpec((None, TM_WSPLIT, n), lambda l, i: (l, i, 0))],
        out_specs=[pl.BlockSpec((None, TM_WSPLIT, MLA_IN), lambda l, i: (l, i, 0)),
                   pl.BlockSpec((None, TM_WSPLIT, n_ret), lambda l, i: (l, i, 0))],
        out_shape=[jax.ShapeDtypeStruct((depth, d, MLA_IN), BF16),
                   jax.ShapeDtypeStruct((depth, d, n_ret), BF16)],
        compiler_params=_cparams(("parallel", "parallel")),
        name="split_w_in",
    )(w)


def _prep_w_uq(w):
    w4 = w.reshape(DEPTH, Q_LORA, MLA_HEADS, NOPE + ROPE)
    wqn = w4[..., :NOPE].reshape(DEPTH, Q_LORA, MLA_HEADS * NOPE).astype(BF16)
    wqr = w4[..., NOPE:].reshape(DEPTH, Q_LORA, MLA_HEADS * ROPE).astype(BF16)
    return wqn, wqr


def _prep_w_ukv(w):
    w4 = w.reshape(DEPTH, KV_LORA, MLA_HEADS, NOPE + VDIM)
    wk = w4[..., :NOPE].reshape(DEPTH, KV_LORA, MLA_HEADS * NOPE).astype(BF16)
    wvt = jnp.swapaxes(w4[..., NOPE:].reshape(DEPTH, KV_LORA, MLA_HEADS * VDIM), 1, 2).astype(BF16)
    return wk, wvt


def kernel(x, positions, attn_norm, w_in, q_norm, kv_norm, w_uq, w_ukv, beta_attn, beta_ret,
           w_o, mlp_norm, w_up, w_down, final_norm):
    b, s, d = x.shape
    t = b * s
    c128, s128, c64, s64 = _rope_tables(positions)
    c128, s128, c64, s64 = (a.reshape(b, s, LANE) for a in (c128, s128, c64, s64))
    dec, xi, zeta, cd = _retention_constants()

    row = lambda g: g.reshape(DEPTH, 1, -1)
    w_mla, w_ret = _prep_w_in(w_in)
    wqn, wqr = _prep_w_uq(w_uq)
    wk, wvt = _prep_w_ukv(w_ukv)
    w_o_b, w_up_b, w_down_b = w_o.astype(BF16), w_up.astype(BF16), w_down.astype(BF16)
    fg = final_norm.reshape(1, d)

    x2 = x.reshape(t, d)
    for l in range(DEPTH):
        proj3 = _inproj(x2, row(attn_norm), w_mla, w_ret, l).reshape(b, s, PROJ_W)
        q3, k3, vt5 = _mla_up(proj3, row(q_norm), row(kv_norm), wqn, wqr, wk, wvt, c64, s64, l)
        a3 = _attention(q3, k3, vt5)
        r3 = _retention(proj3, c128, s128, dec, xi, zeta, cd, row(beta_ret), l)
        x2 = _outproj(a3.reshape(t, -1), r3.reshape(t, -1), x2, row(beta_attn), w_o_b, l)
        x2 = _mlp(x2, row(mlp_norm), w_up_b, w_down_b, fg, l, final=(l == DEPTH - 1))
    return x2.reshape(b, s, d)
```

```python
import functools

import jax
import jax.numpy as jnp
from jax import lax
from jax.experimental import pallas as pl
from jax.experimental.pallas import tpu as pltpu

F32 = jnp.float32
BF16 = jnp.bfloat16

D_MODEL = 2048
DEPTH = 4
MLA_HEADS = 8
NOPE = 128
ROPE = 64
VDIM = 128
Q_LORA = 512
KV_LORA = 256
RET_HEADS = 8
RET_DK = 128
RET_DV = 128
D_FF = 4 * D_MODEL
ROPE_BASE = 10000.0
NORM_EPS = 1e-6
GN_EPS = 1e-6
LOG2E = 1.4426950408889634

LANE = 128
QK_PAD = 256
V_AUG = VDIM + 16
MXU_N = 256
MLA_IN = 1024
PROJ_W = MLA_IN + 4 * RET_HEADS * RET_DK
RET_COL0 = MLA_IN // LANE

VMEM_LIMIT = 56 * 1024 * 1024

TM_PROJ = 1024
TN_PROJ = MLA_IN
TK = 512
TM_MLA = TK
TQ = 1024
TQ_SUB = 256
TM_RET = 512
RET_GROUP = 8
TM_OUT = 512
TM_MLP = 512
TF_MLP = 1024
TM_TAB = 2048


def _cparams(sem, flags=None):
    return pltpu.CompilerParams(dimension_semantics=sem, vmem_limit_bytes=VMEM_LIMIT, flags=flags)


def _rms(x, g):
    ms = jnp.mean(x * x, axis=-1, keepdims=True)
    return x * lax.rsqrt(ms + NORM_EPS) * g


def _rope_table_kernel(pos_ref, f128_ref, s128_ref, f64_ref, s64_ref,
                       c128_ref, sn128_ref, c64_ref, sn64_ref):
    pos = pos_ref[...].astype(F32)
    a128 = pos * f128_ref[...]
    c128_ref[...] = jnp.cos(a128)
    sn128_ref[...] = jnp.sin(a128) * s128_ref[...]
    a64 = pos * f64_ref[...]
    c64_ref[...] = jnp.cos(a64)
    sn64_ref[...] = jnp.sin(a64) * s64_ref[...]


def _rope_tables(positions):
    t = positions.size
    pos = positions.reshape(t, 1)
    inv128 = ROPE_BASE ** (-jnp.arange(0, RET_DK, 2, dtype=F32) / RET_DK)
    inv64 = ROPE_BASE ** (-jnp.arange(0, ROPE, 2, dtype=F32) / ROPE)
    f128 = jnp.concatenate([inv128, inv128]).reshape(1, LANE)
    s128 = jnp.concatenate([-jnp.ones(64, F32), jnp.ones(64, F32)]).reshape(1, LANE)
    f64 = jnp.tile(jnp.concatenate([inv64, inv64]), 2).reshape(1, LANE)
    s64 = jnp.tile(jnp.concatenate([-jnp.ones(32, F32), jnp.ones(32, F32)]), 2).reshape(1, LANE)
    row = pl.BlockSpec((1, LANE), lambda i: (0, 0))
    tab = pl.BlockSpec((TM_TAB, LANE), lambda i: (i, 0))
    shp = jax.ShapeDtypeStruct((t, LANE), F32)
    return pl.pallas_call(
        _rope_table_kernel,
        grid=(t // TM_TAB,),
        in_specs=[pl.BlockSpec((TM_TAB, 1), lambda i: (i, 0)), row, row, row, row],
        out_specs=[tab, tab, tab, tab],
        out_shape=[shp, shp, shp, shp],
        compiler_params=_cparams(("parallel",)),
        name="rope_tables",
    )(pos, f128, s128, f64, s64)


def _inproj_kernel(x_ref, g_ref, wm_ref, wr_ref, o_ref, h_ref):
    j = pl.program_id(1)

    @pl.when(j == 0)
    def _():
        h = _rms(x_ref[...], g_ref[...]).astype(BF16)
        h_ref[...] = h
        o_ref[...] = jnp.dot(h, wm_ref[...], preferred_element_type=F32).astype(o_ref.dtype)

    @pl.when(j > 0)
    def _():
        o_ref[...] = jnp.dot(h_ref[...], wr_ref[...], preferred_element_type=F32).astype(o_ref.dtype)


def _inproj(x2, g, wm, wr, l):
    t = x2.shape[0]
    return pl.pallas_call(
        _inproj_kernel,
        grid=(t // TM_PROJ, PROJ_W // TN_PROJ),
        in_specs=[
            pl.BlockSpec((TM_PROJ, D_MODEL), lambda i, j: (i, 0)),
            pl.BlockSpec((None, 1, D_MODEL), lambda i, j: (l, 0, 0)),
            pl.BlockSpec((None, D_MODEL, MLA_IN), lambda i, j: (l, 0, 0)),
            pl.BlockSpec((None, D_MODEL, TN_PROJ), lambda i, j: (l, 0, jnp.maximum(j - 1, 0))),
        ],
        out_specs=pl.BlockSpec((TM_PROJ, TN_PROJ), lambda i, j: (i, j)),
        out_shape=jax.ShapeDtypeStruct((t, PROJ_W), BF16),
        scratch_shapes=[pltpu.VMEM((TM_PROJ, D_MODEL), BF16)],
        compiler_params=_cparams(("parallel", "arbitrary")),
        name="in_proj",
    )(x2, g, wm, wr)


def _swap32(x):
    lane = lax.broadcasted_iota(jnp.int32, x.shape, 1)
    return jnp.where((lane & 32) == 0, pltpu.roll(x, 96, 1), pltpu.roll(x, 32, 1))


def _mla_up_kernel(cq_ref, ckv_ref, kr_ref, qg_ref, kvg_ref, wqn_ref, wqr_ref, wk_ref, wvt_ref,
                   c64_ref, s64_ref, q_ref, k_ref, vt_ref):
    scale = (NOPE + ROPE) ** -0.5 * LOG2E
    cqn = _rms(cq_ref[...].astype(F32), qg_ref[...]).astype(BF16)
    ckvn = _rms(ckv_ref[...].astype(F32), kvg_ref[...]).astype(BF16)
    cos = c64_ref[...]
    sin = s64_ref[...]
    lane = lax.broadcasted_iota(jnp.int32, cos.shape, 1)
    low = lane < ROPE

    qn = jnp.dot(cqn, wqn_ref[...], preferred_element_type=F32)
    qr = jnp.dot(cqn, wqr_ref[...], preferred_element_type=F32)
    kn = jnp.dot(ckvn, wk_ref[...], preferred_element_type=F32)
    vt = lax.dot_general(wvt_ref[...], ckvn, (((1,), (1,)), ((), ())),
                         preferred_element_type=F32)

    kr = kr_ref[...].astype(F32)
    krr = (kr * cos + _swap32(kr) * sin).astype(BF16)

    for h in range(MLA_HEADS):
        q_ref[:, h * QK_PAD:h * QK_PAD + NOPE] = (qn[:, h * NOPE:(h + 1) * NOPE] * scale).astype(BF16)
        k_ref[:, h * QK_PAD:h * QK_PAD + NOPE] = kn[:, h * NOPE:(h + 1) * NOPE].astype(BF16)
        k_ref[:, h * QK_PAD + NOPE:(h + 1) * QK_PAD] = krr
        vt_ref[h, :VDIM, :] = vt[h * VDIM:(h + 1) * VDIM, :].astype(BF16)
        vt_ref[h, VDIM:, :] = jnp.ones((V_AUG - VDIM, vt.shape[1]), BF16)
    for p in range(MLA_HEADS // 2):
        x = qr[:, p * LANE:(p + 1) * LANE]
        xr = (x * cos + _swap32(x) * sin) * scale
        even = jnp.where(low, xr, 0.0)
        odd = jnp.where(low, pltpu.roll(xr, ROPE, 1), 0.0)
        q_ref[:, (2 * p) * QK_PAD + NOPE:(2 * p + 1) * QK_PAD] = even.astype(BF16)
        q_ref[:, (2 * p + 1) * QK_PAD + NOPE:(2 * p + 2) * QK_PAD] = odd.astype(BF16)


def _mla_up(proj3, qg, kvg, wqn, wqr, wk, wvt, c64, s64, l):
    b, s, _ = proj3.shape
    nc = s // TK
    full = lambda shape: pl.BlockSpec((None,) + shape, lambda bi, i: (l,) + (0,) * len(shape))
    return pl.pallas_call(
        _mla_up_kernel,
        grid=(b, nc),
        in_specs=[
            pl.BlockSpec((None, TM_MLA, Q_LORA), lambda bi, i: (bi, i, 0)),
            pl.BlockSpec((None, TM_MLA, KV_LORA), lambda bi, i: (bi, i, Q_LORA // KV_LORA)),
            pl.BlockSpec((None, TM_MLA, LANE), lambda bi, i: (bi, i, (Q_LORA + KV_LORA) // LANE)),
            full((1, Q_LORA)), full((1, KV_LORA)),
            full((Q_LORA, MLA_HEADS * NOPE)), full((Q_LORA, MLA_HEADS * ROPE)),
            full((KV_LORA, MLA_HEADS * NOPE)), full((MLA_HEADS * VDIM, KV_LORA)),
            pl.BlockSpec((None, TM_MLA, LANE), lambda bi, i: (bi, i, 0)),
            pl.BlockSpec((None, TM_MLA, LANE), lambda bi, i: (bi, i, 0)),
        ],
        out_specs=[
            pl.BlockSpec((None, TM_MLA, MLA_HEADS * QK_PAD), lambda bi, i: (bi, i, 0)),
            pl.BlockSpec((None, TM_MLA, MLA_HEADS * QK_PAD), lambda bi, i: (bi, i, 0)),
            pl.BlockSpec((None, MLA_HEADS, None, V_AUG, TK), lambda bi, i: (bi, 0, i, 0, 0)),
        ],
        out_shape=[
            jax.ShapeDtypeStruct((b, s, MLA_HEADS * QK_PAD), BF16),
            jax.ShapeDtypeStruct((b, s, MLA_HEADS * QK_PAD), BF16),
            jax.ShapeDtypeStruct((b, MLA_HEADS, nc, V_AUG, TK), BF16),
        ],
        compiler_params=_cparams(("parallel", "parallel")),
        name="mla_up",
    )(proj3, proj3, proj3, qg, kvg, wqn, wqr, wk, wvt, c64, s64)


def _attn_kernel(q_ref, k_ref, vt_ref, o_ref, qt_ref, s_ref, cmax_ref, m_ref, acc_ref):
    nq = q_ref.shape[0] // TQ

    def load_qt(i, qslot):
        rows = pl.ds(pl.multiple_of(i * TQ, TQ), TQ)
        qt_ref[qslot] = q_ref[rows, :].astype(F32).T.astype(BF16)

    def scores(j, slot, qslot, q_lo=0):
        kc = k_ref[pl.ds(pl.multiple_of(j * TK, TK), TK), :]
        st = jnp.dot(kc, qt_ref[qslot, :, q_lo:], preferred_element_type=F32)
        s_ref[slot, :, q_lo:TQ] = st
        if q_lo == 0:
            cmax_ref[slot] = jnp.max(st, axis=0, keepdims=True)

    def update(j, slot, diag):
        for c in range(TQ // TQ_SUB):
            q_lo = c * TQ_SUB
            cols = slice(q_lo, q_lo + TQ_SUB)
            nk, masked, k_lo = TK, False, 0
            if diag is not None:
                k_lo = diag * TK
                if k_lo >= q_lo + TQ_SUB:
                    continue
                nk = min(TK, q_lo + TQ_SUB - k_lo)
                masked = k_lo + nk - 1 > q_lo
            st = s_ref[slot, :nk, cols]
            if masked:
                row = lax.broadcasted_iota(jnp.int32, st.shape, 0) + k_lo
                col = lax.broadcasted_iota(jnp.int32, st.shape, 1) + q_lo
                st = jnp.where(row <= col, st, -jnp.inf)
            m_old = m_ref[:, cols]
            cmax = cmax_ref[slot][:, cols] if diag is None else jnp.max(st, axis=0, keepdims=True)
            m_new = jnp.maximum(m_old, cmax)
            alpha = jnp.exp2(m_old - m_new)
            p = jnp.exp2(st - m_new).astype(BF16)
            pv = jnp.dot(vt_ref[j][:, :nk], p, preferred_element_type=F32)
            acc_ref[:, cols] = alpha * acc_ref[:, cols] + pv
            m_ref[:, cols] = m_new

    band = TQ // TK

    def tile(i, qslot):
        m_ref[...] = jnp.full(m_ref.shape, -jnp.inf, F32)
        acc_ref[...] = jnp.zeros(acc_ref.shape, F32)
        first_diag = band * i

        def body(t, carry):
            j = 2 * t
            scores(j + 1, 1, qslot)
            update(j, 0, None)
            scores(j + 2, 0, qslot)
            update(j + 1, 1, None)
            return carry

        lax.fori_loop(0, i * (band // 2), body, 0)

        for d in range(band):
            if d + 1 < band:
                scores(first_diag + d + 1, (d + 1) % 2, qslot, (d + 1) * TK)
            else:
                nxt = jnp.minimum(i + 1, nq - 1)
                load_qt(nxt, 1 - qslot)
                scores(0, 0, 1 - qslot)
            update(first_diag + d, d % 2, d)

        rows = pl.ds(pl.multiple_of(i * TQ, TQ), TQ)
        o_ref[rows, :] = (acc_ref[:VDIM, :] / acc_ref[VDIM:VDIM + 1, :]).T

    load_qt(0, 0)
    scores(0, 0, 0)

    def tile_pair(u, carry):
        tile(2 * u, 0)
        tile(2 * u + 1, 1)
        return carry

    lax.fori_loop(0, nq // 2, tile_pair, 0)


def _attention(q3, k3, vt5):
    b, s, _ = q3.shape
    return pl.pallas_call(
        _attn_kernel,
        grid=(b, MLA_HEADS),
        in_specs=[
            pl.BlockSpec((None, s, QK_PAD), lambda bi, h: (bi, 0, h)),
            pl.BlockSpec((None, s, QK_PAD), lambda bi, h: (bi, 0, h)),
            pl.BlockSpec((None, None, s // TK, V_AUG, TK), lambda bi, h: (bi, h, 0, 0, 0)),
        ],
        out_specs=pl.BlockSpec((None, s, VDIM), lambda bi, h: (bi, 0, h)),
        out_shape=jax.ShapeDtypeStruct((b, s, MLA_HEADS * VDIM), F32),
        scratch_shapes=[
            pltpu.VMEM((2, QK_PAD, TQ), BF16),
            pltpu.VMEM((2, TK, TQ + LANE), F32),
            pltpu.VMEM((2, 1, TQ), F32),
            pltpu.VMEM((1, TQ), F32),
            pltpu.VMEM((V_AUG, TQ), F32),
        ],
        compiler_params=_cparams(("parallel", "parallel")),
        name="mla_attention",
    )(q3, k3, vt5)


def _ret_kernel(rq_ref, rk_ref, rv_ref, rg_ref, c_ref, s_ref, dec_ref, xi_ref, zeta_ref, cd_ref, beta_ref,
                o_ref, state_ref):
    @pl.when(pl.program_id(2) == 0)
    def _():
        state_ref[...] = jnp.zeros(state_ref.shape, F32)

    cos = c_ref[...]
    sin = s_ref[...]
    for g in range(RET_GROUP):
        cols = slice(g * RET_DK, (g + 1) * RET_DK)
        rq = rq_ref[:, cols].astype(F32)
        rk = rk_ref[:, cols].astype(F32)
        v = rv_ref[:, cols]
        q = (rq * cos + pltpu.roll(rq, RET_DK // 2, 1) * sin) * (RET_DK ** -0.5)
        k = rk * cos + pltpu.roll(rk, RET_DK // 2, 1) * sin
        scores = lax.dot_general(q.astype(BF16), k.astype(BF16), (((1,), (1,)), ((), ())),
                                 preferred_element_type=F32) * dec_ref[g]
        st = state_ref[g]
        o = (jnp.dot(scores.astype(BF16), v, preferred_element_type=F32)
             + jnp.dot((q * xi_ref[g]).astype(BF16), st.astype(BF16), preferred_element_type=F32))
        kz = (k * zeta_ref[g]).astype(BF16)
        state_ref[g] = cd_ref[g] * st + lax.dot_general(kz, v, (((0,), (0,)), ((), ())),
                                                        preferred_element_type=F32)
        mu = jnp.mean(o, axis=-1, keepdims=True)
        d = o - mu
        var = jnp.mean(d * d, axis=-1, keepdims=True)
        on = d * lax.rsqrt(var + GN_EPS)
        gate = rg_ref[:, cols].astype(F32)
        silu = gate * (1.0 / (1.0 + jnp.exp(-gate)))
        o_ref[:, cols] = (on * silu * beta_ref[:, cols]).astype(o_ref.dtype)


def _retention(proj3, c128, s128, dec, xi, zeta, cd, beta, l):
    b, s, _ = proj3.shape
    gw = RET_GROUP * RET_DK
    ngroups = RET_HEADS // RET_GROUP
    seg = lambda k: pl.BlockSpec((None, TM_RET, gw),
                                 lambda bi, h, i: (bi, i, (RET_COL0 + k * RET_HEADS) // RET_GROUP + h))
    tab = pl.BlockSpec((None, TM_RET, LANE), lambda bi, h, i: (bi, i, 0))
    per_head = lambda rows, width: pl.BlockSpec((RET_GROUP, rows, width), lambda bi, h, i: (h, 0, 0))
    return pl.pallas_call(
        _ret_kernel,
        grid=(b, ngroups, s // TM_RET),
        in_specs=[seg(0), seg(1), seg(2), seg(3), tab, tab,
                  per_head(TM_RET, TM_RET), per_head(TM_RET, LANE), per_head(TM_RET, LANE), per_head(RET_DK, LANE),
                  pl.BlockSpec((None, 1, gw), lambda bi, h, i: (l, 0, h))],
        out_specs=pl.BlockSpec((None, TM_RET, gw), lambda bi, h, i: (bi, i, h)),
        out_shape=jax.ShapeDtypeStruct((b, s, RET_HEADS * RET_DV), BF16),
        scratch_shapes=[pltpu.VMEM((RET_GROUP, RET_DK, RET_DV), F32)],
        compiler_params=_cparams(("parallel", "parallel", "arbitrary")),
        name="retention",
    )(proj3, proj3, proj3, proj3, c128, s128, dec, xi, zeta, cd, beta)


def _retention_constants():
    h, c = RET_HEADS, TM_RET
    log_gamma = jnp.log1p(-jnp.exp2(-5.0 - jnp.arange(h, dtype=F32)))
    idx = jnp.arange(c, dtype=F32)
    rel = idx[:, None] - idx[None, :]
    dec = jnp.where(rel >= 0, jnp.exp(log_gamma[:, None, None] * jnp.maximum(rel, 0.0)), 0.0)
    xi = jnp.exp(log_gamma[:, None] * (idx + 1.0))
    zeta = jnp.exp(log_gamma[:, None] * (c - 1.0 - idx))
    cd = jnp.exp(log_gamma * c)
    bc = lambda t: jnp.broadcast_to(t[:, :, None], (h, c, LANE))
    return dec, bc(xi), bc(zeta), jnp.broadcast_to(cd[:, None, None], (h, RET_DK, LANE))


def _outproj_kernel(a_ref, r_ref, x_ref, ba_ref, wt_ref, wb_ref, o_ref):
    an = _rms(a_ref[...], ba_ref[...]).astype(BF16)
    y = jnp.dot(an, wt_ref[...], preferred_element_type=F32)
    y = y + jnp.dot(r_ref[...], wb_ref[...], preferred_element_type=F32)
    o_ref[...] = x_ref[...] + y


def _outproj(a2, r2, x2, ba, wo, l):
    t = x2.shape[0]
    half = MLA_HEADS * VDIM
    return pl.pallas_call(
        _outproj_kernel,
        grid=(t // TM_OUT,),
        in_specs=[
            pl.BlockSpec((TM_OUT, half), lambda i: (i, 0)),
            pl.BlockSpec((TM_OUT, half), lambda i: (i, 0)),
            pl.BlockSpec((TM_OUT, D_MODEL), lambda i: (i, 0)),
            pl.BlockSpec((None, 1, half), lambda i: (l, 0, 0)),
            pl.BlockSpec((None, half, D_MODEL), lambda i: (l, 0, 0)),
            pl.BlockSpec((None, half, D_MODEL), lambda i: (l, 1, 0)),
        ],
        out_specs=pl.BlockSpec((TM_OUT, D_MODEL), lambda i: (i, 0)),
        out_shape=jax.ShapeDtypeStruct((t, D_MODEL), F32),
        compiler_params=_cparams(("parallel",)),
        name="out_proj",
    )(a2, r2, x2, ba, wo, wo)


def _mlp_kernel(x_ref, g_ref, wu_ref, wd_ref, fg_ref, o_ref, h_ref, *, final):
    j = pl.program_id(1)

    @pl.when(j == 0)
    def _():
        x = x_ref[...]
        h_ref[...] = _rms(x, g_ref[...]).astype(BF16)
        o_ref[...] = x

    u = jnp.maximum(jnp.dot(h_ref[...], wu_ref[...], preferred_element_type=F32), 0.0)
    u = (u * u).astype(BF16)
    o_ref[...] += jnp.dot(u, wd_ref[...], preferred_element_type=F32)

    if final:
        @pl.when(j == pl.num_programs(1) - 1)
        def _():
            o_ref[...] = _rms(o_ref[...], fg_ref[...])


def _mlp(x2, g, wu, wd, fg, l, final):
    t = x2.shape[0]
    return pl.pallas_call(
        functools.partial(_mlp_kernel, final=final),
        grid=(t // TM_MLP, D_FF // TF_MLP),
        in_specs=[
            pl.BlockSpec((TM_MLP, D_MODEL), lambda i, j: (i, 0)),
            pl.BlockSpec((None, 1, D_MODEL), lambda i, j: (l, 0, 0)),
            pl.BlockSpec((None, D_MODEL, TF_MLP), lambda i, j: (l, 0, j)),
            pl.BlockSpec((None, TF_MLP, D_MODEL), lambda i, j: (l, j, 0)),
            pl.BlockSpec((1, D_MODEL), lambda i, j: (0, 0)),
        ],
        out_specs=pl.BlockSpec((TM_MLP, D_MODEL), lambda i, j: (i, 0)),
        out_shape=jax.ShapeDtypeStruct((t, D_MODEL), F32),
        scratch_shapes=[pltpu.VMEM((TM_MLP, D_MODEL), BF16)],
        compiler_params=_cparams(("parallel", "arbitrary")),
        name="mlp",
    )(x2, g, wu, wd, fg)


def _prep_w_in(w):
    cut = Q_LORA + KV_LORA + ROPE
    pad = jnp.zeros(w.shape[:2] + (MLA_IN - cut,), BF16)
    wm = jnp.concatenate([w[..., :cut].astype(BF16), pad], axis=-1)
    return wm, w[..., cut:].astype(BF16)


def _prep_w_uq(w):
    w4 = w.reshape(DEPTH, Q_LORA, MLA_HEADS, NOPE + ROPE)
    wqn = w4[..., :NOPE].reshape(DEPTH, Q_LORA, MLA_HEADS * NOPE).astype(BF16)
    wqr = w4[..., NOPE:].reshape(DEPTH, Q_LORA, MLA_HEADS * ROPE).astype(BF16)
    return wqn, wqr


def _prep_w_ukv(w):
    w4 = w.reshape(DEPTH, KV_LORA, MLA_HEADS, NOPE + VDIM)
    wk = w4[..., :NOPE].reshape(DEPTH, KV_LORA, MLA_HEADS * NOPE).astype(BF16)
    wvt = jnp.swapaxes(w4[..., NOPE:].reshape(DEPTH, KV_LORA, MLA_HEADS * VDIM), 1, 2).astype(BF16)
    return wk, wvt


def kernel(x, positions, attn_norm, w_in, q_norm, kv_norm, w_uq, w_ukv, beta_attn, beta_ret,
           w_o, mlp_norm, w_up, w_down, final_norm):
    b, s, d = x.shape
    t = b * s
    c128, s128, c64, s64 = _rope_tables(positions)
    c128, s128, c64, s64 = (a.reshape(b, s, LANE) for a in (c128, s128, c64, s64))
    dec, xi, zeta, cd = _retention_constants()

    row = lambda g: g.reshape(DEPTH, 1, -1)
    w_mla, w_ret = _prep_w_in(w_in)
    wqn, wqr = _prep_w_uq(w_uq)
    wk, wvt = _prep_w_ukv(w_ukv)
    w_o_b, w_up_b, w_down_b = w_o.astype(BF16), w_up.astype(BF16), w_down.astype(BF16)
    fg = final_norm.reshape(1, d)

    x2 = x.reshape(t, d)
    for l in range(DEPTH):
        proj3 = _inproj(x2, row(attn_norm), w_mla, w_ret, l).reshape(b, s, PROJ_W)
        q3, k3, vt5 = _mla_up(proj3, row(q_norm), row(kv_norm), wqn, wqr, wk, wvt, c64, s64, l)
        a3 = _attention(q3, k3, vt5)
        r3 = _retention(proj3, c128, s128, dec, xi, zeta, cd, row(beta_ret), l)
        x2 = _outproj(a3.reshape(t, -1), r3.reshape(t, -1), x2, row(beta_attn), w_o_b, l)
        x2 = _mlp(x2, row(mlp_norm), w_up_b, w_down_b, fg, l, final=(l == DEPTH - 1))
    return x2.reshape(b, s, d)
```

```python
import functools

import jax
import jax.numpy as jnp
from jax import lax
from jax.experimental import pallas as pl
from jax.experimental.pallas import tpu as pltpu

F32 = jnp.float32
BF16 = jnp.bfloat16

D_MODEL = 2048
DEPTH = 4
MLA_HEADS = 8
NOPE = 128
ROPE = 64
VDIM = 128
Q_LORA = 512
KV_LORA = 256
RET_HEADS = 8
RET_DK = 128
RET_DV = 128
D_FF = 4 * D_MODEL
ROPE_BASE = 10000.0
NORM_EPS = 1e-6
GN_EPS = 1e-6
LOG2E = 1.4426950408889634

LANE = 128
QK_PAD = 256
V_AUG = VDIM + 16
MXU_N = 256
MLA_IN = 1024
PROJ_W = MLA_IN + 4 * RET_HEADS * RET_DK
RET_COL0 = MLA_IN // LANE

VMEM_LIMIT = 56 * 1024 * 1024

TM_PROJ = 1024
TN_PROJ = MLA_IN
TK = 512
TM_MLA = TK
TQ = 1024
TQ_SUB = 256
TM_RET = 512
RET_GROUP = 8
TM_OUT = 512
TM_MLP = 512
TF_MLP = 1024
TM_TAB = 2048


def _cparams(sem, flags=None):
    return pltpu.CompilerParams(dimension_semantics=sem, vmem_limit_bytes=VMEM_LIMIT, flags=flags)


def _rms(x, g):
    ms = jnp.mean(x * x, axis=-1, keepdims=True)
    return x * lax.rsqrt(ms + NORM_EPS) * g


def _rope_table_kernel(pos_ref, f128_ref, s128_ref, f64_ref, s64_ref,
                       c128_ref, sn128_ref, c64_ref, sn64_ref):
    pos = pos_ref[...].astype(F32)
    a128 = pos * f128_ref[...]
    c128_ref[...] = jnp.cos(a128)
    sn128_ref[...] = jnp.sin(a128) * s128_ref[...]
    a64 = pos * f64_ref[...]
    c64_ref[...] = jnp.cos(a64)
    sn64_ref[...] = jnp.sin(a64) * s64_ref[...]


def _rope_tables(positions):
    t = positions.size
    pos = positions.reshape(t, 1)
    inv128 = ROPE_BASE ** (-jnp.arange(0, RET_DK, 2, dtype=F32) / RET_DK)
    inv64 = ROPE_BASE ** (-jnp.arange(0, ROPE, 2, dtype=F32) / ROPE)
    f128 = jnp.concatenate([inv128, inv128]).reshape(1, LANE)
    s128 = jnp.concatenate([-jnp.ones(64, F32), jnp.ones(64, F32)]).reshape(1, LANE)
    f64 = jnp.tile(jnp.concatenate([inv64, inv64]), 2).reshape(1, LANE)
    s64 = jnp.tile(jnp.concatenate([-jnp.ones(32, F32), jnp.ones(32, F32)]), 2).reshape(1, LANE)
    row = pl.BlockSpec((1, LANE), lambda i: (0, 0))
    tab = pl.BlockSpec((TM_TAB, LANE), lambda i: (i, 0))
    shp = jax.ShapeDtypeStruct((t, LANE), F32)
    return pl.pallas_call(
        _rope_table_kernel,
        grid=(t // TM_TAB,),
        in_specs=[pl.BlockSpec((TM_TAB, 1), lambda i: (i, 0)), row, row, row, row],
        out_specs=[tab, tab, tab, tab],
        out_shape=[shp, shp, shp, shp],
        compiler_params=_cparams(("parallel",)),
        name="rope_tables",
    )(pos, f128, s128, f64, s64)


def _inproj_kernel(x_ref, g_ref, wm_ref, wr_ref, o_ref, h_ref):
    j = pl.program_id(1)

    @pl.when(j == 0)
    def _():
        h = _rms(x_ref[...], g_ref[...]).astype(BF16)
        h_ref[...] = h
        o_ref[...] = jnp.dot(h, wm_ref[...], preferred_element_type=F32).astype(o_ref.dtype)

    @pl.when(j > 0)
    def _():
        o_ref[...] = jnp.dot(h_ref[...], wr_ref[...], preferred_element_type=F32).astype(o_ref.dtype)


def _inproj(x2, g, wm, wr, l):
    t = x2.shape[0]
    return pl.pallas_call(
        _inproj_kernel,
        grid=(t // TM_PROJ, PROJ_W // TN_PROJ),
        in_specs=[
            pl.BlockSpec((TM_PROJ, D_MODEL), lambda i, j: (i, 0)),
            pl.BlockSpec((None, 1, D_MODEL), lambda i, j: (l, 0, 0)),
            pl.BlockSpec((None, D_MODEL, MLA_IN), lambda i, j: (l, 0, 0)),
            pl.BlockSpec((None, D_MODEL, TN_PROJ), lambda i, j: (l, 0, jnp.maximum(j - 1, 0))),
        ],
        out_specs=pl.BlockSpec((TM_PROJ, TN_PROJ), lambda i, j: (i, j)),
        out_shape=jax.ShapeDtypeStruct((t, PROJ_W), BF16),
        scratch_shapes=[pltpu.VMEM((TM_PROJ, D_MODEL), BF16)],
        compiler_params=_cparams(("parallel", "arbitrary")),
        name="in_proj",
    )(x2, g, wm, wr)


def _swap32(x):
    lane = lax.broadcasted_iota(jnp.int32, x.shape, 1)
    return jnp.where((lane & 32) == 0, pltpu.roll(x, 96, 1), pltpu.roll(x, 32, 1))


def _mla_up_kernel(cq_ref, ckv_ref, kr_ref, qg_ref, kvg_ref, wqn_ref, wqr_ref, wk_ref, wvt_ref,
                   c64_ref, s64_ref, q_ref, k_ref, vt_ref):
    scale = (NOPE + ROPE) ** -0.5 * LOG2E
    cqn = _rms(cq_ref[...].astype(F32), qg_ref[...]).astype(BF16)
    ckvn = _rms(ckv_ref[...].astype(F32), kvg_ref[...]).astype(BF16)
    cos = c64_ref[...]
    sin = s64_ref[...]
    lane = lax.broadcasted_iota(jnp.int32, cos.shape, 1)
    low = lane < ROPE

    qn = jnp.dot(cqn, wqn_ref[...], preferred_element_type=F32)
    qr = jnp.dot(cqn, wqr_ref[...], preferred_element_type=F32)
    kn = jnp.dot(ckvn, wk_ref[...], preferred_element_type=F32)
    vt = lax.dot_general(wvt_ref[...], ckvn, (((1,), (1,)), ((), ())),
                         preferred_element_type=F32)

    kr = kr_ref[...].astype(F32)
    krr = (kr * cos + _swap32(kr) * sin).astype(BF16)

    for h in range(MLA_HEADS):
        q_ref[:, h * QK_PAD:h * QK_PAD + NOPE] = (qn[:, h * NOPE:(h + 1) * NOPE] * scale).astype(BF16)
        k_ref[:, h * QK_PAD:h * QK_PAD + NOPE] = kn[:, h * NOPE:(h + 1) * NOPE].astype(BF16)
        k_ref[:, h * QK_PAD + NOPE:(h + 1) * QK_PAD] = krr
        vt_ref[h, :VDIM, :] = vt[h * VDIM:(h + 1) * VDIM, :].astype(BF16)
        vt_ref[h, VDIM:, :] = jnp.ones((V_AUG - VDIM, vt.shape[1]), BF16)
    for p in range(MLA_HEADS // 2):
        x = qr[:, p * LANE:(p + 1) * LANE]
        xr = (x * cos + _swap32(x) * sin) * scale
        even = jnp.where(low, xr, 0.0)
        odd = jnp.where(low, pltpu.roll(xr, ROPE, 1), 0.0)
        q_ref[:, (2 * p) * QK_PAD + NOPE:(2 * p + 1) * QK_PAD] = even.astype(BF16)
        q_ref[:, (2 * p + 1) * QK_PAD + NOPE:(2 * p + 2) * QK_PAD] = odd.astype(BF16)


def _mla_up(proj3, qg, kvg, wqn, wqr, wk, wvt, c64, s64, l):
    b, s, _ = proj3.shape
    nc = s // TK
    full = lambda shape: pl.BlockSpec((None,) + shape, lambda bi, i: (l,) + (0,) * len(shape))
    return pl.pallas_call(
        _mla_up_kernel,
        grid=(b, nc),
        in_specs=[
            pl.BlockSpec((None, TM_MLA, Q_LORA), lambda bi, i: (bi, i, 0)),
            pl.BlockSpec((None, TM_MLA, KV_LORA), lambda bi, i: (bi, i, Q_LORA // KV_LORA)),
            pl.BlockSpec((None, TM_MLA, LANE), lambda bi, i: (bi, i, (Q_LORA + KV_LORA) // LANE)),
            full((1, Q_LORA)), full((1, KV_LORA)),
            full((Q_LORA, MLA_HEADS * NOPE)), full((Q_LORA, MLA_HEADS * ROPE)),
            full((KV_LORA, MLA_HEADS * NOPE)), full((MLA_HEADS * VDIM, KV_LORA)),
            pl.BlockSpec((None, TM_MLA, LANE), lambda bi, i: (bi, i, 0)),
            pl.BlockSpec((None, TM_MLA, LANE), lambda bi, i: (bi, i, 0)),
        ],
        out_specs=[
            pl.BlockSpec((None, TM_MLA, MLA_HEADS * QK_PAD), lambda bi, i: (bi, i, 0)),
            pl.BlockSpec((None, TM_MLA, MLA_HEADS * QK_PAD), lambda bi, i: (bi, i, 0)),
            pl.BlockSpec((None, MLA_HEADS, None, V_AUG, TK), lambda bi, i: (bi, 0, i, 0, 0)),
        ],
        out_shape=[
            jax.ShapeDtypeStruct((b, s, MLA_HEADS * QK_PAD), BF16),
            jax.ShapeDtypeStruct((b, s, MLA_HEADS * QK_PAD), BF16),
            jax.ShapeDtypeStruct((b, MLA_HEADS, nc, V_AUG, TK), BF16),
        ],
        compiler_params=_cparams(("parallel", "parallel")),
        name="mla_up",
    )(proj3, proj3, proj3, qg, kvg, wqn, wqr, wk, wvt, c64, s64)


def _attn_kernel(q_ref, k_ref, vt_ref, o_ref, qt_ref, s_ref, cmax_ref, m_ref, acc_ref):
    nq = q_ref.shape[0] // TQ

    def load_qt(i, qslot):
        rows = pl.ds(pl.multiple_of(i * TQ, TQ), TQ)
        qt_ref[qslot] = q_ref[rows, :].astype(F32).T.astype(BF16)

    def scores(j, slot, qslot, q_lo=0):
        kc = k_ref[pl.ds(pl.multiple_of(j * TK, TK), TK), :]
        st = jnp.dot(kc, qt_ref[qslot, :, q_lo:], preferred_element_type=F32)
        s_ref[slot, :, q_lo:TQ] = st
        if q_lo == 0:
            cmax_ref[slot] = jnp.max(st, axis=0, keepdims=True)

    def update(j, slot, diag):
        for c in range(TQ // TQ_SUB):
            q_lo = c * TQ_SUB
            cols = slice(q_lo, q_lo + TQ_SUB)
            nk, masked, k_lo = TK, False, 0
            if diag is not None:
                k_lo = diag * TK
                if k_lo >= q_lo + TQ_SUB:
                    continue
                nk = min(TK, q_lo + TQ_SUB - k_lo)
                masked = k_lo + nk - 1 > q_lo
            st = s_ref[slot, :nk, cols]
            if masked:
                row = lax.broadcasted_iota(jnp.int32, st.shape, 0) + k_lo
                col = lax.broadcasted_iota(jnp.int32, st.shape, 1) + q_lo
                st = jnp.where(row <= col, st, -jnp.inf)
            m_old = m_ref[:, cols]
            cmax = cmax_ref[slot][:, cols] if diag is None else jnp.max(st, axis=0, keepdims=True)
            m_new = jnp.maximum(m_old, cmax)
            alpha = jnp.exp2(m_old - m_new)
            p = jnp.exp2(st - m_new).astype(BF16)
            pv = jnp.dot(vt_ref[j][:, :nk], p, preferred_element_type=F32)
            acc_ref[:, cols] = alpha * acc_ref[:, cols] + pv
            m_ref[:, cols] = m_new

    band = TQ // TK
    assert band == 2 and nq % 2 == 0

    def tile(i, qslot):
        m_ref[...] = jnp.full(m_ref.shape, -jnp.inf, F32)
        acc_ref[...] = jnp.zeros(acc_ref.shape, F32)
        first_diag = band * i

        def pair(j):
            scores(j + 1, 1, qslot)
            update(j, 0, None)
            scores(j + 2, 0, qslot)
            update(j + 1, 1, None)

        def body(t, carry):
            pair(4 * t)
            pair(4 * t + 2)
            return carry

        lax.fori_loop(0, i // 2, body, 0)
        if qslot == 1:
            pair(first_diag - 2)

        for d in range(band):
            if d + 1 < band:
                scores(first_diag + d + 1, (d + 1) % 2, qslot, (d + 1) * TK)
            else:
                nxt = jnp.minimum(i + 1, nq - 1)
                load_qt(nxt, 1 - qslot)
                scores(0, 0, 1 - qslot)
            update(first_diag + d, d % 2, d)

        rows = pl.ds(pl.multiple_of(i * TQ, TQ), TQ)
        o_ref[rows, :] = (acc_ref[:VDIM, :] / acc_ref[VDIM:VDIM + 1, :]).T

    load_qt(0, 0)
    scores(0, 0, 0)

    def tile_pair(u, carry):
        tile(2 * u, 0)
        tile(2 * u + 1, 1)
        return carry

    lax.fori_loop(0, nq // 2, tile_pair, 0)


def _attention(q3, k3, vt5):
    b, s, _ = q3.shape
    return pl.pallas_call(
        _attn_kernel,
        grid=(b, MLA_HEADS),
        in_specs=[
            pl.BlockSpec((None, s, QK_PAD), lambda bi, h: (bi, 0, h)),
            pl.BlockSpec((None, s, QK_PAD), lambda bi, h: (bi, 0, h)),
            pl.BlockSpec((None, None, s // TK, V_AUG, TK), lambda bi, h: (bi, h, 0, 0, 0)),
        ],
        out_specs=pl.BlockSpec((None, s, VDIM), lambda bi, h: (bi, 0, h)),
        out_shape=jax.ShapeDtypeStruct((b, s, MLA_HEADS * VDIM), F32),
        scratch_shapes=[
            pltpu.VMEM((2, QK_PAD, TQ), BF16),
            pltpu.VMEM((2, TK, TQ + LANE), F32),
            pltpu.VMEM((2, 1, TQ), F32),
            pltpu.VMEM((1, TQ), F32),
            pltpu.VMEM((V_AUG, TQ), F32),
        ],
        compiler_params=_cparams(("parallel", "parallel")),
        name="mla_attention",
    )(q3, k3, vt5)


def _ret_kernel(rq_ref, rk_ref, rv_ref, rg_ref, c_ref, s_ref, dec_ref, xi_ref, zeta_ref, cd_ref, beta_ref,
                o_ref, state_ref):
    @pl.when(pl.program_id(2) == 0)
    def _():
        state_ref[...] = jnp.zeros(state_ref.shape, F32)

    cos = c_ref[...]
    sin = s_ref[...]
    for g in range(RET_GROUP):
        cols = slice(g * RET_DK, (g + 1) * RET_DK)
        rq = rq_ref[:, cols].astype(F32)
        rk = rk_ref[:, cols].astype(F32)
        v = rv_ref[:, cols]
        q = (rq * cos + pltpu.roll(rq, RET_DK // 2, 1) * sin) * (RET_DK ** -0.5)
        k = rk * cos + pltpu.roll(rk, RET_DK // 2, 1) * sin
        scores = lax.dot_general(q.astype(BF16), k.astype(BF16), (((1,), (1,)), ((), ())),
                                 preferred_element_type=F32) * dec_ref[g]
        st = state_ref[g]
        o = (jnp.dot(scores.astype(BF16), v, preferred_element_type=F32)
             + jnp.dot((q * xi_ref[g]).astype(BF16), st.astype(BF16), preferred_element_type=F32))
        kz = (k * zeta_ref[g]).astype(BF16)
        state_ref[g] = cd_ref[g] * st + lax.dot_general(kz, v, (((0,), (0,)), ((), ())),
                                                        preferred_element_type=F32)
        mu = jnp.mean(o, axis=-1, keepdims=True)
        d = o - mu
        var = jnp.mean(d * d, axis=-1, keepdims=True)
        on = d * lax.rsqrt(var + GN_EPS)
        gate = rg_ref[:, cols].astype(F32)
        silu = gate * (1.0 / (1.0 + jnp.exp(-gate)))
        o_ref[:, cols] = (on * silu * beta_ref[:, cols]).astype(o_ref.dtype)


def _retention(proj3, c128, s128, dec, xi, zeta, cd, beta, l):
    b, s, _ = proj3.shape
    gw = RET_GROUP * RET_DK
    ngroups = RET_HEADS // RET_GROUP
    seg = lambda k: pl.BlockSpec((None, TM_RET, gw),
                                 lambda bi, h, i: (bi, i, (RET_COL0 + k * RET_HEADS) // RET_GROUP + h))
    tab = pl.BlockSpec((None, TM_RET, LANE), lambda bi, h, i: (bi, i, 0))
    per_head = lambda rows, width: pl.BlockSpec((RET_GROUP, rows, width), lambda bi, h, i: (h, 0, 0))
    return pl.pallas_call(
        _ret_kernel,
        grid=(b, ngroups, s // TM_RET),
        in_specs=[seg(0), seg(1), seg(2), seg(3), tab, tab,
                  per_head(TM_RET, TM_RET), per_head(TM_RET, LANE), per_head(TM_RET, LANE), per_head(RET_DK, LANE),
                  pl.BlockSpec((None, 1, gw), lambda bi, h, i: (l, 0, h))],
        out_specs=pl.BlockSpec((None, TM_RET, gw), lambda bi, h, i: (bi, i, h)),
        out_shape=jax.ShapeDtypeStruct((b, s, RET_HEADS * RET_DV), BF16),
        scratch_shapes=[pltpu.VMEM((RET_GROUP, RET_DK, RET_DV), F32)],
        compiler_params=_cparams(("parallel", "parallel", "arbitrary")),
        name="retention",
    )(proj3, proj3, proj3, proj3, c128, s128, dec, xi, zeta, cd, beta)


def _retention_constants():
    h, c = RET_HEADS, TM_RET
    log_gamma = jnp.log1p(-jnp.exp2(-5.0 - jnp.arange(h, dtype=F32)))
    idx = jnp.arange(c, dtype=F32)
    rel = idx[:, None] - idx[None, :]
    dec = jnp.where(rel >= 0, jnp.exp(log_gamma[:, None, None] * jnp.maximum(rel, 0.0)), 0.0)
    xi = jnp.exp(log_gamma[:, None] * (idx + 1.0))
    zeta = jnp.exp(log_gamma[:, None] * (c - 1.0 - idx))
    cd = jnp.exp(log_gamma * c)
    bc = lambda t: jnp.broadcast_to(t[:, :, None], (h, c, LANE))
    return dec, bc(xi), bc(zeta), jnp.broadcast_to(cd[:, None, None], (h, RET_DK, LANE))


def _outproj_kernel(a_ref, r_ref, x_ref, ba_ref, wt_ref, wb_ref, o_ref):
    an = _rms(a_ref[...], ba_ref[...]).astype(BF16)
    y = jnp.dot(an, wt_ref[...], preferred_element_type=F32)
    y = y + jnp.dot(r_ref[...], wb_ref[...], preferred_element_type=F32)
    o_ref[...] = x_ref[...] + y


def _outproj(a2, r2, x2, ba, wo, l):
    t = x2.shape[0]
    half = MLA_HEADS * VDIM
    return pl.pallas_call(
        _outproj_kernel,
        grid=(t // TM_OUT,),
        in_specs=[
            pl.BlockSpec((TM_OUT, half), lambda i: (i, 0)),
            pl.BlockSpec((TM_OUT, half), lambda i: (i, 0)),
            pl.BlockSpec((TM_OUT, D_MODEL), lambda i: (i, 0)),
            pl.BlockSpec((None, 1, half), lambda i: (l, 0, 0)),
            pl.BlockSpec((None, half, D_MODEL), lambda i: (l, 0, 0)),
            pl.BlockSpec((None, half, D_MODEL), lambda i: (l, 1, 0)),
        ],
        out_specs=pl.BlockSpec((TM_OUT, D_MODEL), lambda i: (i, 0)),
        out_shape=jax.ShapeDtypeStruct((t, D_MODEL), F32),
        compiler_params=_cparams(("parallel",)),
        name="out_proj",
    )(a2, r2, x2, ba, wo, wo)


def _mlp_kernel(x_ref, g_ref, wu_ref, wd_ref, fg_ref, o_ref, h_ref, *, final):
    j = pl.program_id(1)

    @pl.when(j == 0)
    def _():
        x = x_ref[...]
        h_ref[...] = _rms(x, g_ref[...]).astype(BF16)
        o_ref[...] = x

    u = jnp.maximum(jnp.dot(h_ref[...], wu_ref[...], preferred_element_type=F32), 0.0)
    u = (u * u).astype(BF16)
    o_ref[...] += jnp.dot(u, wd_ref[...], preferred_element_type=F32)

    if final:
        @pl.when(j == pl.num_programs(1) - 1)
        def _():
            o_ref[...] = _rms(o_ref[...], fg_ref[...])


def _mlp(x2, g, wu, wd, fg, l, final):
    t = x2.shape[0]
    return pl.pallas_call(
        functools.partial(_mlp_kernel, final=final),
        grid=(t // TM_MLP, D_FF // TF_MLP),
        in_specs=[
            pl.BlockSpec((TM_MLP, D_MODEL), lambda i, j: (i, 0)),
            pl.BlockSpec((None, 1, D_MODEL), lambda i, j: (l, 0, 0)),
            pl.BlockSpec((None, D_MODEL, TF_MLP), lambda i, j: (l, 0, j)),
            pl.BlockSpec((None, TF_MLP, D_MODEL), lambda i, j: (l, j, 0)),
            pl.BlockSpec((1, D_MODEL), lambda i, j: (0, 0)),
        ],
        out_specs=pl.BlockSpec((TM_MLP, D_MODEL), lambda i, j: (i, 0)),
        out_shape=jax.ShapeDtypeStruct((t, D_MODEL), F32),
        scratch_shapes=[pltpu.VMEM((TM_MLP, D_MODEL), BF16)],
        compiler_params=_cparams(("parallel", "arbitrary")),
        name="mlp",
    )(x2, g, wu, wd, fg)


def _prep_w_in(w):
    cut = Q_LORA + KV_LORA + ROPE
    pad = jnp.zeros(w.shape[:2] + (MLA_IN - cut,), BF16)
    wm = jnp.concatenate([w[..., :cut].astype(BF16), pad], axis=-1)
    return wm, w[..., cut:].astype(BF16)


def _prep_w_uq(w):
    w4 = w.reshape(DEPTH, Q_LORA, MLA_HEADS, NOPE + ROPE)
    wqn = w4[..., :NOPE].reshape(DEPTH, Q_LORA, MLA_HEADS * NOPE).astype(BF16)
    wqr = w4[..., NOPE:].reshape(DEPTH, Q_LORA, MLA_HEADS * ROPE).astype(BF16)
    return wqn, wqr


def _prep_w_ukv(w):
    w4 = w.reshape(DEPTH, KV_LORA, MLA_HEADS, NOPE + VDIM)
    wk = w4[..., :NOPE].reshape(DEPTH, KV_LORA, MLA_HEADS * NOPE).astype(BF16)
    wvt = jnp.swapaxes(w4[..., NOPE:].reshape(DEPTH, KV_LORA, MLA_HEADS * VDIM), 1, 2).astype(BF16)
    return wk, wvt


def kernel(x, positions, attn_norm, w_in, q_norm, kv_norm, w_uq, w_ukv, beta_attn, beta_ret,
           w_o, mlp_norm, w_up, w_down, final_norm):
    b, s, d = x.shape
    t = b * s
    c128, s128, c64, s64 = _rope_tables(positions)
    c128, s128, c64, s64 = (a.reshape(b, s, LANE) for a in (c128, s128, c64, s64))
    dec, xi, zeta, cd = _retention_constants()

    row = lambda g: g.reshape(DEPTH, 1, -1)
    w_mla, w_ret = _prep_w_in(w_in)
    wqn, wqr = _prep_w_uq(w_uq)
    wk, wvt = _prep_w_ukv(w_ukv)
    w_o_b, w_up_b, w_down_b = w_o.astype(BF16), w_up.astype(BF16), w_down.astype(BF16)
    fg = final_norm.reshape(1, d)

    x2 = x.reshape(t, d)
    for l in range(DEPTH):
        proj3 = _inproj(x2, row(attn_norm), w_mla, w_ret, l).reshape(b, s, PROJ_W)
        q3, k3, vt5 = _mla_up(proj3, row(q_norm), row(kv_norm), wqn, wqr, wk, wvt, c64, s64, l)
        a3 = _attention(q3, k3, vt5)
        r3 = _retention(proj3, c128, s128, dec, xi, zeta, cd, row(beta_ret), l)
        x2 = _outproj(a3.reshape(t, -1), r3.reshape(t, -1), x2, row(beta_attn), w_o_b, l)
        x2 = _mlp(x2, row(mlp_norm), w_up_b, w_down_b, fg, l, final=(l == DEPTH - 1))
    return x2.reshape(b, s, d)
```

```python
import functools

import jax
import jax.numpy as jnp
from jax import lax
from jax.experimental import pallas as pl
from jax.experimental.pallas import tpu as pltpu

F32 = jnp.float32
BF16 = jnp.bfloat16

D_MODEL = 2048
DEPTH = 4
MLA_HEADS = 8
NOPE = 128
ROPE = 64
VDIM = 128
Q_LORA = 512
KV_LORA = 256
RET_HEADS = 8
RET_DK = 128
RET_DV = 128
D_FF = 4 * D_MODEL
ROPE_BASE = 10000.0
NORM_EPS = 1e-6
GN_EPS = 1e-6
LOG2E = 1.4426950408889634

LANE = 128
QK_PAD = 256
V_AUG = VDIM + 16
MXU_N = 256
MLA_IN = 1024
PROJ_W = MLA_IN + 4 * RET_HEADS * RET_DK
RET_COL0 = MLA_IN // LANE

VMEM_LIMIT = 56 * 1024 * 1024

TM_PROJ = 1024
TN_PROJ = MLA_IN
TK = 512
TM_MLA = 2 * TK
TQ = 1024
TQ_SUB = 256
TM_RET = 512
RET_GROUP = 8
TM_OUT = 512
TM_MLP = 512
TF_MLP = 1024
TM_TAB = 2048


def _cparams(sem, flags=None):
    return pltpu.CompilerParams(dimension_semantics=sem, vmem_limit_bytes=VMEM_LIMIT, flags=flags)


def _rms(x, g):
    ms = jnp.mean(x * x, axis=-1, keepdims=True)
    return x * lax.rsqrt(ms + NORM_EPS) * g


def _rope_table_kernel(pos_ref, f128_ref, s128_ref, f64_ref, s64_ref,
                       c128_ref, sn128_ref, c64_ref, sn64_ref):
    pos = pos_ref[...].astype(F32)
    a128 = pos * f128_ref[...]
    c128_ref[...] = jnp.cos(a128)
    sn128_ref[...] = jnp.sin(a128) * s128_ref[...]
    a64 = pos * f64_ref[...]
    c64_ref[...] = jnp.cos(a64)
    sn64_ref[...] = jnp.sin(a64) * s64_ref[...]


def _rope_tables(positions):
    t = positions.size
    pos = positions.reshape(t, 1)
    inv128 = ROPE_BASE ** (-jnp.arange(0, RET_DK, 2, dtype=F32) / RET_DK)
    inv64 = ROPE_BASE ** (-jnp.arange(0, ROPE, 2, dtype=F32) / ROPE)
    f128 = jnp.concatenate([inv128, inv128]).reshape(1, LANE)
    s128 = jnp.concatenate([-jnp.ones(64, F32), jnp.ones(64, F32)]).reshape(1, LANE)
    f64 = jnp.tile(jnp.concatenate([inv64, inv64]), 2).reshape(1, LANE)
    s64 = jnp.tile(jnp.concatenate([-jnp.ones(32, F32), jnp.ones(32, F32)]), 2).reshape(1, LANE)
    row = pl.BlockSpec((1, LANE), lambda i: (0, 0))
    tab = pl.BlockSpec((TM_TAB, LANE), lambda i: (i, 0))
    shp = jax.ShapeDtypeStruct((t, LANE), F32)
    return pl.pallas_call(
        _rope_table_kernel,
        grid=(t // TM_TAB,),
        in_specs=[pl.BlockSpec((TM_TAB, 1), lambda i: (i, 0)), row, row, row, row],
        out_specs=[tab, tab, tab, tab],
        out_shape=[shp, shp, shp, shp],
        compiler_params=_cparams(("parallel",)),
        name="rope_tables",
    )(pos, f128, s128, f64, s64)


def _inproj_kernel(x_ref, g_ref, wm_ref, wr_ref, o_ref, h_ref):
    j = pl.program_id(1)

    @pl.when(j == 0)
    def _():
        h = _rms(x_ref[...], g_ref[...]).astype(BF16)
        h_ref[...] = h
        o_ref[...] = jnp.dot(h, wm_ref[...], preferred_element_type=F32).astype(o_ref.dtype)

    @pl.when(j > 0)
    def _():
        o_ref[...] = jnp.dot(h_ref[...], wr_ref[...], preferred_element_type=F32).astype(o_ref.dtype)


def _inproj(x2, g, wm, wr, l):
    t = x2.shape[0]
    return pl.pallas_call(
        _inproj_kernel,
        grid=(t // TM_PROJ, PROJ_W // TN_PROJ),
        in_specs=[
            pl.BlockSpec((TM_PROJ, D_MODEL), lambda i, j: (i, 0)),
            pl.BlockSpec((None, 1, D_MODEL), lambda i, j: (l, 0, 0)),
            pl.BlockSpec((None, D_MODEL, MLA_IN), lambda i, j: (l, 0, 0)),
            pl.BlockSpec((None, D_MODEL, TN_PROJ), lambda i, j: (l, 0, jnp.maximum(j - 1, 0))),
        ],
        out_specs=pl.BlockSpec((TM_PROJ, TN_PROJ), lambda i, j: (i, j)),
        out_shape=jax.ShapeDtypeStruct((t, PROJ_W), BF16),
        scratch_shapes=[pltpu.VMEM((TM_PROJ, D_MODEL), BF16)],
        compiler_params=_cparams(("parallel", "arbitrary")),
        name="in_proj",
    )(x2, g, wm, wr)


def _swap32(x):
    lane = lax.broadcasted_iota(jnp.int32, x.shape, 1)
    return jnp.where((lane & 32) == 0, pltpu.roll(x, 96, 1), pltpu.roll(x, 32, 1))


def _mla_up_kernel(cq_ref, ckv_ref, kr_ref, qg_ref, kvg_ref, wqn_ref, wqr_ref, wk_ref, wvt_ref,
                   c64_ref, s64_ref, q_ref, k_ref, vt_ref):
    scale = (NOPE + ROPE) ** -0.5 * LOG2E
    cqn = _rms(cq_ref[...].astype(F32), qg_ref[...]).astype(BF16)
    ckvn = _rms(ckv_ref[...].astype(F32), kvg_ref[...]).astype(BF16)
    cos = c64_ref[...]
    sin = s64_ref[...]
    lane = lax.broadcasted_iota(jnp.int32, cos.shape, 1)
    low = lane < ROPE

    qn = jnp.dot(cqn, wqn_ref[...], preferred_element_type=F32)
    qr = jnp.dot(cqn, wqr_ref[...], preferred_element_type=F32)
    kn = jnp.dot(ckvn, wk_ref[...], preferred_element_type=F32)
    vt = lax.dot_general(wvt_ref[...], ckvn, (((1,), (1,)), ((), ())),
                         preferred_element_type=F32)

    kr = kr_ref[...].astype(F32)
    krr = (kr * cos + _swap32(kr) * sin).astype(BF16)

    for h in range(MLA_HEADS):
        q_ref[:, h * QK_PAD:h * QK_PAD + NOPE] = (qn[:, h * NOPE:(h + 1) * NOPE] * scale).astype(BF16)
        k_ref[:, h * QK_PAD:h * QK_PAD + NOPE] = kn[:, h * NOPE:(h + 1) * NOPE].astype(BF16)
        k_ref[:, h * QK_PAD + NOPE:(h + 1) * QK_PAD] = krr
        for c in range(TM_MLA // TK):
            vt_ref[h, c, :VDIM, :] = vt[h * VDIM:(h + 1) * VDIM, c * TK:(c + 1) * TK].astype(BF16)
            vt_ref[h, c, VDIM:, :] = jnp.ones((V_AUG - VDIM, TK), BF16)
    for p in range(MLA_HEADS // 2):
        x = qr[:, p * LANE:(p + 1) * LANE]
        xr = (x * cos + _swap32(x) * sin) * scale
        even = jnp.where(low, xr, 0.0)
        odd = jnp.where(low, pltpu.roll(xr, ROPE, 1), 0.0)
        q_ref[:, (2 * p) * QK_PAD + NOPE:(2 * p + 1) * QK_PAD] = even.astype(BF16)
        q_ref[:, (2 * p + 1) * QK_PAD + NOPE:(2 * p + 2) * QK_PAD] = odd.astype(BF16)


def _mla_up(proj3, qg, kvg, wqn, wqr, wk, wvt, c64, s64, l):
    b, s, _ = proj3.shape
    nc = s // TK
    full = lambda shape: pl.BlockSpec((None,) + shape, lambda bi, i: (l,) + (0,) * len(shape))
    return pl.pallas_call(
        _mla_up_kernel,
        grid=(b, s // TM_MLA),
        in_specs=[
            pl.BlockSpec((None, TM_MLA, Q_LORA), lambda bi, i: (bi, i, 0)),
            pl.BlockSpec((None, TM_MLA, KV_LORA), lambda bi, i: (bi, i, Q_LORA // KV_LORA)),
            pl.BlockSpec((None, TM_MLA, LANE), lambda bi, i: (bi, i, (Q_LORA + KV_LORA) // LANE)),
            full((1, Q_LORA)), full((1, KV_LORA)),
            full((Q_LORA, MLA_HEADS * NOPE)), full((Q_LORA, MLA_HEADS * ROPE)),
            full((KV_LORA, MLA_HEADS * NOPE)), full((MLA_HEADS * VDIM, KV_LORA)),
            pl.BlockSpec((None, TM_MLA, LANE), lambda bi, i: (bi, i, 0)),
            pl.BlockSpec((None, TM_MLA, LANE), lambda bi, i: (bi, i, 0)),
        ],
        out_specs=[
            pl.BlockSpec((None, TM_MLA, MLA_HEADS * QK_PAD), lambda bi, i: (bi, i, 0)),
            pl.BlockSpec((None, TM_MLA, MLA_HEADS * QK_PAD), lambda bi, i: (bi, i, 0)),
            pl.BlockSpec((None, MLA_HEADS, TM_MLA // TK, V_AUG, TK), lambda bi, i: (bi, 0, i, 0, 0)),
        ],
        out_shape=[
            jax.ShapeDtypeStruct((b, s, MLA_HEADS * QK_PAD), BF16),
            jax.ShapeDtypeStruct((b, s, MLA_HEADS * QK_PAD), BF16),
            jax.ShapeDtypeStruct((b, MLA_HEADS, nc, V_AUG, TK), BF16),
        ],
        compiler_params=_cparams(("parallel", "parallel")),
        name="mla_up",
    )(proj3, proj3, proj3, qg, kvg, wqn, wqr, wk, wvt, c64, s64)


def _attn_kernel(q_ref, k_ref, vt_ref, o_ref, qt_ref, s_ref, cmax_ref, m_ref, acc_ref):
    nq = q_ref.shape[0] // TQ

    def load_qt(i, qslot):
        rows = pl.ds(pl.multiple_of(i * TQ, TQ), TQ)
        qt_ref[qslot] = q_ref[rows, :].astype(F32).T.astype(BF16)

    def scores(j, slot, qslot, q_lo=0):
        kc = k_ref[pl.ds(pl.multiple_of(j * TK, TK), TK), :]
        st = jnp.dot(kc, qt_ref[qslot, :, q_lo:], preferred_element_type=F32)
        s_ref[slot, :, q_lo:TQ] = st
        if q_lo == 0:
            cmax_ref[slot] = jnp.max(st, axis=0, keepdims=True)

    def update(j, slot, diag):
        for c in range(TQ // TQ_SUB):
            q_lo = c * TQ_SUB
            cols = slice(q_lo, q_lo + TQ_SUB)
            nk, masked, k_lo = TK, False, 0
            if diag is not None:
                k_lo = diag * TK
                if k_lo >= q_lo + TQ_SUB:
                    continue
                nk = min(TK, q_lo + TQ_SUB - k_lo)
                masked = k_lo + nk - 1 > q_lo
            st = s_ref[slot, :nk, cols]
            if masked:
                row = lax.broadcasted_iota(jnp.int32, st.shape, 0) + k_lo
                col = lax.broadcasted_iota(jnp.int32, st.shape, 1) + q_lo
                st = jnp.where(row <= col, st, -jnp.inf)
            m_old = m_ref[:, cols]
            cmax = cmax_ref[slot][:, cols] if diag is None else jnp.max(st, axis=0, keepdims=True)
            m_new = jnp.maximum(m_old, cmax)
            alpha = jnp.exp2(m_old - m_new)
            p = jnp.exp2(st - m_new).astype(BF16)
            pv = jnp.dot(vt_ref[j][:, :nk], p, preferred_element_type=F32)
            acc_ref[:, cols] = alpha * acc_ref[:, cols] + pv
            m_ref[:, cols] = m_new

    band = TQ // TK
    assert band == 2 and nq % 2 == 0

    def tile(i, qslot):
        m_ref[...] = jnp.full(m_ref.shape, -jnp.inf, F32)
        acc_ref[...] = jnp.zeros(acc_ref.shape, F32)
        first_diag = band * i

        def pair(j):
            scores(j + 1, 1, qslot)
            update(j, 0, None)
            scores(j + 2, 0, qslot)
            update(j + 1, 1, None)

        def body(t, carry):
            pair(4 * t)
            pair(4 * t + 2)
            return carry

        lax.fori_loop(0, i // 2, body, 0)
        if qslot == 1:
            pair(first_diag - 2)

        for d in range(band):
            if d + 1 < band:
                scores(first_diag + d + 1, (d + 1) % 2, qslot, (d + 1) * TK)
            else:
                nxt = jnp.minimum(i + 1, nq - 1)
                load_qt(nxt, 1 - qslot)
                scores(0, 0, 1 - qslot)
            update(first_diag + d, d % 2, d)

        rows = pl.ds(pl.multiple_of(i * TQ, TQ), TQ)
        o_ref[rows, :] = (acc_ref[:VDIM, :] / acc_ref[VDIM:VDIM + 1, :]).T

    load_qt(0, 0)
    scores(0, 0, 0)

    def tile_pair(u, carry):
        tile(2 * u, 0)
        tile(2 * u + 1, 1)
        return carry

    lax.fori_loop(0, nq // 2, tile_pair, 0)


def _attention(q3, k3, vt5):
    b, s, _ = q3.shape
    return pl.pallas_call(
        _attn_kernel,
        grid=(b, MLA_HEADS),
        in_specs=[
            pl.BlockSpec((None, s, QK_PAD), lambda bi, h: (bi, 0, h)),
            pl.BlockSpec((None, s, QK_PAD), lambda bi, h: (bi, 0, h)),
            pl.BlockSpec((None, None, s // TK, V_AUG, TK), lambda bi, h: (bi, h, 0, 0, 0)),
        ],
        out_specs=pl.BlockSpec((None, s, VDIM), lambda bi, h: (bi, 0, h)),
        out_shape=jax.ShapeDtypeStruct((b, s, MLA_HEADS * VDIM), F32),
        scratch_shapes=[
            pltpu.VMEM((2, QK_PAD, TQ), BF16),
            pltpu.VMEM((2, TK, TQ + LANE), F32),
            pltpu.VMEM((2, 1, TQ), F32),
            pltpu.VMEM((1, TQ), F32),
            pltpu.VMEM((V_AUG, TQ), F32),
        ],
        compiler_params=_cparams(("parallel", "parallel")),
        name="mla_attention",
    )(q3, k3, vt5)


def _ret_kernel(rq_ref, rk_ref, rv_ref, rg_ref, c_ref, s_ref, dec_ref, xi_ref, zeta_ref, cd_ref, beta_ref,
                o_ref, state_ref):
    @pl.when(pl.program_id(2) == 0)
    def _():
        state_ref[...] = jnp.zeros(state_ref.shape, F32)

    cos = c_ref[...]
    sin = s_ref[...]
    for g in range(RET_GROUP):
        cols = slice(g * RET_DK, (g + 1) * RET_DK)
        rq = rq_ref[:, cols].astype(F32)
        rk = rk_ref[:, cols].astype(F32)
        v = rv_ref[:, cols]
        q = (rq * cos + pltpu.roll(rq, RET_DK // 2, 1) * sin) * (RET_DK ** -0.5)
        k = rk * cos + pltpu.roll(rk, RET_DK // 2, 1) * sin
        scores = lax.dot_general(q.astype(BF16), k.astype(BF16), (((1,), (1,)), ((), ())),
                                 preferred_element_type=F32) * dec_ref[g]
        st = state_ref[g]
        o = (jnp.dot(scores.astype(BF16), v, preferred_element_type=F32)
             + jnp.dot((q * xi_ref[g]).astype(BF16), st.astype(BF16), preferred_element_type=F32))
        kz = (k * zeta_ref[g]).astype(BF16)
        state_ref[g] = cd_ref[g] * st + lax.dot_general(kz, v, (((0,), (0,)), ((), ())),
                                                        preferred_element_type=F32)
        mu = jnp.mean(o, axis=-1, keepdims=True)
        d = o - mu
        var = jnp.mean(d * d, axis=-1, keepdims=True)
        on = d * lax.rsqrt(var + GN_EPS)
        gate = rg_ref[:, cols].astype(F32)
        silu = gate * (1.0 / (1.0 + jnp.exp(-gate)))
        o_ref[:, cols] = (on * silu * beta_ref[:, cols]).astype(o_ref.dtype)


def _retention(proj3, c128, s128, dec, xi, zeta, cd, beta, l):
    b, s, _ = proj3.shape
    gw = RET_GROUP * RET_DK
    ngroups = RET_HEADS // RET_GROUP
    seg = lambda k: pl.BlockSpec((None, TM_RET, gw),
                                 lambda bi, h, i: (bi, i, (RET_COL0 + k * RET_HEADS) // RET_GROUP + h))
    tab = pl.BlockSpec((None, TM_RET, LANE), lambda bi, h, i: (bi, i, 0))
    per_head = lambda rows, width: pl.BlockSpec((RET_GROUP, rows, width), lambda bi, h, i: (h, 0, 0))
    return pl.pallas_call(
        _ret_kernel,
        grid=(b, ngroups, s // TM_RET),
        in_specs=[seg(0), seg(1), seg(2), seg(3), tab, tab,
                  per_head(TM_RET, TM_RET), per_head(TM_RET, LANE), per_head(TM_RET, LANE), per_head(RET_DK, LANE),
                  pl.BlockSpec((None, 1, gw), lambda bi, h, i: (l, 0, h))],
        out_specs=pl.BlockSpec((None, TM_RET, gw), lambda bi, h, i: (bi, i, h)),
        out_shape=jax.ShapeDtypeStruct((b, s, RET_HEADS * RET_DV), BF16),
        scratch_shapes=[pltpu.VMEM((RET_GROUP, RET_DK, RET_DV), F32)],
        compiler_params=_cparams(("parallel", "parallel", "arbitrary")),
        name="retention",
    )(proj3, proj3, proj3, proj3, c128, s128, dec, xi, zeta, cd, beta)


def _retention_constants():
    h, c = RET_HEADS, TM_RET
    log_gamma = jnp.log1p(-jnp.exp2(-5.0 - jnp.arange(h, dtype=F32)))
    idx = jnp.arange(c, dtype=F32)
    rel = idx[:, None] - idx[None, :]
    dec = jnp.where(rel >= 0, jnp.exp(log_gamma[:, None, None] * jnp.maximum(rel, 0.0)), 0.0)
    xi = jnp.exp(log_gamma[:, None] * (idx + 1.0))
    zeta = jnp.exp(log_gamma[:, None] * (c - 1.0 - idx))
    cd = jnp.exp(log_gamma * c)
    bc = lambda t: jnp.broadcast_to(t[:, :, None], (h, c, LANE))
    return dec, bc(xi), bc(zeta), jnp.broadcast_to(cd[:, None, None], (h, RET_DK, LANE))


def _outproj_kernel(a_ref, r_ref, x_ref, ba_ref, wt_ref, wb_ref, o_ref):
    an = _rms(a_ref[...], ba_ref[...]).astype(BF16)
    y = jnp.dot(an, wt_ref[...], preferred_element_type=F32)
    y = y + jnp.dot(r_ref[...], wb_ref[...], preferred_element_type=F32)
    o_ref[...] = x_ref[...] + y


def _outproj(a2, r2, x2, ba, wo, l):
    t = x2.shape[0]
    half = MLA_HEADS * VDIM
    return pl.pallas_call(
        _outproj_kernel,
        grid=(t // TM_OUT,),
        in_specs=[
            pl.BlockSpec((TM_OUT, half), lambda i: (i, 0)),
            pl.BlockSpec((TM_OUT, half), lambda i: (i, 0)),
            pl.BlockSpec((TM_OUT, D_MODEL), lambda i: (i, 0)),
            pl.BlockSpec((None, 1, half), lambda i: (l, 0, 0)),
            pl.BlockSpec((None, half, D_MODEL), lambda i: (l, 0, 0)),
            pl.BlockSpec((None, half, D_MODEL), lambda i: (l, 1, 0)),
        ],
        out_specs=pl.BlockSpec((TM_OUT, D_MODEL), lambda i: (i, 0)),
        out_shape=jax.ShapeDtypeStruct((t, D_MODEL), F32),
        compiler_params=_cparams(("parallel",)),
        name="out_proj",
    )(a2, r2, x2, ba, wo, wo)


def _mlp_kernel(x_ref, g_ref, wu_ref, wd_ref, fg_ref, o_ref, h_ref, *, final):
    j = pl.program_id(1)

    @pl.when(j == 0)
    def _():
        x = x_ref[...]
        h_ref[...] = _rms(x, g_ref[...]).astype(BF16)
        o_ref[...] = x

    u = jnp.maximum(jnp.dot(h_ref[...], wu_ref[...], preferred_element_type=F32), 0.0)
    u = (u * u).astype(BF16)
    o_ref[...] += jnp.dot(u, wd_ref[...], preferred_element_type=F32)

    if final:
        @pl.when(j == pl.num_programs(1) - 1)
        def _():
            o_ref[...] = _rms(o_ref[...], fg_ref[...])


def _mlp(x2, g, wu, wd, fg, l, final):
    t = x2.shape[0]
    return pl.pallas_call(
        functools.partial(_mlp_kernel, final=final),
        grid=(t // TM_MLP, D_FF // TF_MLP),
        in_specs=[
            pl.BlockSpec((TM_MLP, D_MODEL), lambda i, j: (i, 0)),
            pl.BlockSpec((None, 1, D_MODEL), lambda i, j: (l, 0, 0)),
            pl.BlockSpec((None, D_MODEL, TF_MLP), lambda i, j: (l, 0, j)),
            pl.BlockSpec((None, TF_MLP, D_MODEL), lambda i, j: (l, j, 0)),
            pl.BlockSpec((1, D_MODEL), lambda i, j: (0, 0)),
        ],
        out_specs=pl.BlockSpec((TM_MLP, D_MODEL), lambda i, j: (i, 0)),
        out_shape=jax.ShapeDtypeStruct((t, D_MODEL), F32),
        scratch_shapes=[pltpu.VMEM((TM_MLP, D_MODEL), BF16)],
        compiler_params=_cparams(("parallel", "arbitrary")),
        name="mlp",
    )(x2, g, wu, wd, fg)


def _prep_w_in(w):
    cut = Q_LORA + KV_LORA + ROPE
    pad = jnp.zeros(w.shape[:2] + (MLA_IN - cut,), BF16)
    wm = jnp.concatenate([w[..., :cut].astype(BF16), pad], axis=-1)
    return wm, w[..., cut:].astype(BF16)


def _prep_w_uq(w):
    w4 = w.reshape(DEPTH, Q_LORA, MLA_HEADS, NOPE + ROPE)
    wqn = w4[..., :NOPE].reshape(DEPTH, Q_LORA, MLA_HEADS * NOPE).astype(BF16)
    wqr = w4[..., NOPE:].reshape(DEPTH, Q_LORA, MLA_HEADS * ROPE).astype(BF16)
    return wqn, wqr


def _prep_w_ukv(w):
    w4 = w.reshape(DEPTH, KV_LORA, MLA_HEADS, NOPE + VDIM)
    wk = w4[..., :NOPE].reshape(DEPTH, KV_LORA, MLA_HEADS * NOPE).astype(BF16)
    wvt = jnp.swapaxes(w4[..., NOPE:].reshape(DEPTH, KV_LORA, MLA_HEADS * VDIM), 1, 2).astype(BF16)
    return wk, wvt


def kernel(x, positions, attn_norm, w_in, q_norm, kv_norm, w_uq, w_ukv, beta_attn, beta_ret,
           w_o, mlp_norm, w_up, w_down, final_norm):
    b, s, d = x.shape
    t = b * s
    c128, s128, c64, s64 = _rope_tables(positions)
    c128, s128, c64, s64 = (a.reshape(b, s, LANE) for a in (c128, s128, c64, s64))
    dec, xi, zeta, cd = _retention_constants()

    row = lambda g: g.reshape(DEPTH, 1, -1)
    w_mla, w_ret = _prep_w_in(w_in)
    wqn, wqr = _prep_w_uq(w_uq)
    wk, wvt = _prep_w_ukv(w_ukv)
    w_o_b, w_up_b, w_down_b = w_o.astype(BF16), w_up.astype(BF16), w_down.astype(BF16)
    fg = final_norm.reshape(1, d)

    x2 = x.reshape(t, d)
    for l in range(DEPTH):
        proj3 = _inproj(x2, row(attn_norm), w_mla, w_ret, l).reshape(b, s, PROJ_W)
        q3, k3, vt5 = _mla_up(proj3, row(q_norm), row(kv_norm), wqn, wqr, wk, wvt, c64, s64, l)
        a3 = _attention(q3, k3, vt5)
        r3 = _retention(proj3, c128, s128, dec, xi, zeta, cd, row(beta_ret), l)
        x2 = _outproj(a3.reshape(t, -1), r3.reshape(t, -1), x2, row(beta_attn), w_o_b, l)
        x2 = _mlp(x2, row(mlp_norm), w_up_b, w_down_b, fg, l, final=(l == DEPTH - 1))
    return x2.reshape(b, s, d)
```

```python
import functools

import jax
import jax.numpy as jnp
from jax import lax
from jax.experimental import pallas as pl
from jax.experimental.pallas import tpu as pltpu

F32 = jnp.float32
BF16 = jnp.bfloat16

D_MODEL = 2048
DEPTH = 4
MLA_HEADS = 8
NOPE = 128
ROPE = 64
VDIM = 128
Q_LORA = 512
KV_LORA = 256
RET_HEADS = 8
RET_DK = 128
RET_DV = 128
D_FF = 4 * D_MODEL
ROPE_BASE = 10000.0
NORM_EPS = 1e-6
GN_EPS = 1e-6
LOG2E = 1.4426950408889634

LANE = 128
QK_PAD = 256
V_AUG = VDIM + 16
MXU_N = 256
MLA_IN = 1024
PROJ_W = MLA_IN + 4 * RET_HEADS * RET_DK
RET_COL0 = MLA_IN // LANE

VMEM_LIMIT = 56 * 1024 * 1024

TM_PROJ = 1024
TN_PROJ = MLA_IN
TK = 512
TM_MLA = 2 * TK
TQ = 1024
TQ_SUB = 256
TILE_GROUP = 4
TM_RET = 512
RET_GROUP = 8
TM_OUT = 512
TM_MLP = 512
TF_MLP = 1024
TM_TAB = 2048


def _cparams(sem, flags=None):
    return pltpu.CompilerParams(dimension_semantics=sem, vmem_limit_bytes=VMEM_LIMIT, flags=flags)


def _rms(x, g):
    ms = jnp.mean(x * x, axis=-1, keepdims=True)
    return x * lax.rsqrt(ms + NORM_EPS) * g


def _rope_table_kernel(pos_ref, f128_ref, s128_ref, f64_ref, s64_ref,
                       c128_ref, sn128_ref, c64_ref, sn64_ref):
    pos = pos_ref[...].astype(F32)
    a128 = pos * f128_ref[...]
    c128_ref[...] = jnp.cos(a128)
    sn128_ref[...] = jnp.sin(a128) * s128_ref[...]
    a64 = pos * f64_ref[...]
    c64_ref[...] = jnp.cos(a64)
    sn64_ref[...] = jnp.sin(a64) * s64_ref[...]


def _rope_tables(positions):
    t = positions.size
    pos = positions.reshape(t, 1)
    inv128 = ROPE_BASE ** (-jnp.arange(0, RET_DK, 2, dtype=F32) / RET_DK)
    inv64 = ROPE_BASE ** (-jnp.arange(0, ROPE, 2, dtype=F32) / ROPE)
    f128 = jnp.concatenate([inv128, inv128]).reshape(1, LANE)
    s128 = jnp.concatenate([-jnp.ones(64, F32), jnp.ones(64, F32)]).reshape(1, LANE)
    f64 = jnp.tile(jnp.concatenate([inv64, inv64]), 2).reshape(1, LANE)
    s64 = jnp.tile(jnp.concatenate([-jnp.ones(32, F32), jnp.ones(32, F32)]), 2).reshape(1, LANE)
    row = pl.BlockSpec((1, LANE), lambda i: (0, 0))
    tab = pl.BlockSpec((TM_TAB, LANE), lambda i: (i, 0))
    shp = jax.ShapeDtypeStruct((t, LANE), F32)
    return pl.pallas_call(
        _rope_table_kernel,
        grid=(t // TM_TAB,),
        in_specs=[pl.BlockSpec((TM_TAB, 1), lambda i: (i, 0)), row, row, row, row],
        out_specs=[tab, tab, tab, tab],
        out_shape=[shp, shp, shp, shp],
        compiler_params=_cparams(("parallel",)),
        name="rope_tables",
    )(pos, f128, s128, f64, s64)


def _inproj_kernel(x_ref, g_ref, wm_ref, wr_ref, o_ref, h_ref):
    j = pl.program_id(1)

    @pl.when(j == 0)
    def _():
        h = _rms(x_ref[...], g_ref[...]).astype(BF16)
        h_ref[...] = h
        o_ref[...] = jnp.dot(h, wm_ref[...], preferred_element_type=F32).astype(o_ref.dtype)

    @pl.when(j > 0)
    def _():
        o_ref[...] = jnp.dot(h_ref[...], wr_ref[...], preferred_element_type=F32).astype(o_ref.dtype)


def _inproj(x2, g, wm, wr, l):
    t = x2.shape[0]
    return pl.pallas_call(
        _inproj_kernel,
        grid=(t // TM_PROJ, PROJ_W // TN_PROJ),
        in_specs=[
            pl.BlockSpec((TM_PROJ, D_MODEL), lambda i, j: (i, 0)),
            pl.BlockSpec((None, 1, D_MODEL), lambda i, j: (l, 0, 0)),
            pl.BlockSpec((None, D_MODEL, MLA_IN), lambda i, j: (l, 0, 0)),
            pl.BlockSpec((None, D_MODEL, TN_PROJ), lambda i, j: (l, 0, jnp.maximum(j - 1, 0))),
        ],
        out_specs=pl.BlockSpec((TM_PROJ, TN_PROJ), lambda i, j: (i, j)),
        out_shape=jax.ShapeDtypeStruct((t, PROJ_W), BF16),
        scratch_shapes=[pltpu.VMEM((TM_PROJ, D_MODEL), BF16)],
        compiler_params=_cparams(("parallel", "arbitrary")),
        name="in_proj",
    )(x2, g, wm, wr)


def _swap32(x):
    lane = lax.broadcasted_iota(jnp.int32, x.shape, 1)
    return jnp.where((lane & 32) == 0, pltpu.roll(x, 96, 1), pltpu.roll(x, 32, 1))


def _mla_up_kernel(cq_ref, ckv_ref, kr_ref, qg_ref, kvg_ref, wqn_ref, wqr_ref, wk_ref, wvt_ref,
                   c64_ref, s64_ref, q_ref, k_ref, vt_ref):
    scale = (NOPE + ROPE) ** -0.5 * LOG2E
    cqn = _rms(cq_ref[...].astype(F32), qg_ref[...]).astype(BF16)
    ckvn = _rms(ckv_ref[...].astype(F32), kvg_ref[...]).astype(BF16)
    cos = c64_ref[...]
    sin = s64_ref[...]
    lane = lax.broadcasted_iota(jnp.int32, cos.shape, 1)
    low = lane < ROPE

    qn = jnp.dot(cqn, wqn_ref[...], preferred_element_type=F32)
    qr = jnp.dot(cqn, wqr_ref[...], preferred_element_type=F32)
    kn = jnp.dot(ckvn, wk_ref[...], preferred_element_type=F32)
    vt = lax.dot_general(wvt_ref[...], ckvn, (((1,), (1,)), ((), ())),
                         preferred_element_type=F32)

    kr = kr_ref[...].astype(F32)
    krr = (kr * cos + _swap32(kr) * sin).astype(BF16)

    for h in range(MLA_HEADS):
        q_ref[:, h * QK_PAD:h * QK_PAD + NOPE] = (qn[:, h * NOPE:(h + 1) * NOPE] * scale).astype(BF16)
        k_ref[:, h * QK_PAD:h * QK_PAD + NOPE] = kn[:, h * NOPE:(h + 1) * NOPE].astype(BF16)
        k_ref[:, h * QK_PAD + NOPE:(h + 1) * QK_PAD] = krr
        for c in range(TM_MLA // TK):
            vt_ref[h, c, :VDIM, :] = vt[h * VDIM:(h + 1) * VDIM, c * TK:(c + 1) * TK].astype(BF16)
            vt_ref[h, c, VDIM:, :] = jnp.ones((V_AUG - VDIM, TK), BF16)
    for p in range(MLA_HEADS // 2):
        x = qr[:, p * LANE:(p + 1) * LANE]
        xr = (x * cos + _swap32(x) * sin) * scale
        even = jnp.where(low, xr, 0.0)
        odd = jnp.where(low, pltpu.roll(xr, ROPE, 1), 0.0)
        q_ref[:, (2 * p) * QK_PAD + NOPE:(2 * p + 1) * QK_PAD] = even.astype(BF16)
        q_ref[:, (2 * p + 1) * QK_PAD + NOPE:(2 * p + 2) * QK_PAD] = odd.astype(BF16)


def _mla_up(proj3, qg, kvg, wqn, wqr, wk, wvt, c64, s64, l):
    b, s, _ = proj3.shape
    nc = s // TK
    full = lambda shape: pl.BlockSpec((None,) + shape, lambda bi, i: (l,) + (0,) * len(shape))
    return pl.pallas_call(
        _mla_up_kernel,
        grid=(b, s // TM_MLA),
        in_specs=[
            pl.BlockSpec((None, TM_MLA, Q_LORA), lambda bi, i: (bi, i, 0)),
            pl.BlockSpec((None, TM_MLA, KV_LORA), lambda bi, i: (bi, i, Q_LORA // KV_LORA)),
            pl.BlockSpec((None, TM_MLA, LANE), lambda bi, i: (bi, i, (Q_LORA + KV_LORA) // LANE)),
            full((1, Q_LORA)), full((1, KV_LORA)),
            full((Q_LORA, MLA_HEADS * NOPE)), full((Q_LORA, MLA_HEADS * ROPE)),
            full((KV_LORA, MLA_HEADS * NOPE)), full((MLA_HEADS * VDIM, KV_LORA)),
            pl.BlockSpec((None, TM_MLA, LANE), lambda bi, i: (bi, i, 0)),
            pl.BlockSpec((None, TM_MLA, LANE), lambda bi, i: (bi, i, 0)),
        ],
        out_specs=[
            pl.BlockSpec((None, TM_MLA, MLA_HEADS * QK_PAD), lambda bi, i: (bi, i, 0)),
            pl.BlockSpec((None, TM_MLA, MLA_HEADS * QK_PAD), lambda bi, i: (bi, i, 0)),
            pl.BlockSpec((None, MLA_HEADS, TM_MLA // TK, V_AUG, TK), lambda bi, i: (bi, 0, i, 0, 0)),
        ],
        out_shape=[
            jax.ShapeDtypeStruct((b, s, MLA_HEADS * QK_PAD), BF16),
            jax.ShapeDtypeStruct((b, s, MLA_HEADS * QK_PAD), BF16),
            jax.ShapeDtypeStruct((b, MLA_HEADS, nc, V_AUG, TK), BF16),
        ],
        compiler_params=_cparams(("parallel", "parallel")),
        name="mla_up",
    )(proj3, proj3, proj3, qg, kvg, wqn, wqr, wk, wvt, c64, s64)


def _attn_kernel(q_ref, k_ref, vt_ref, o_ref, qt_ref, s_ref, cmax_ref, m_ref, acc_ref):
    nq = q_ref.shape[0] // TQ

    def load_qt(i, qslot):
        rows = pl.ds(pl.multiple_of(i * TQ, TQ), TQ)
        qt_ref[qslot] = q_ref[rows, :].astype(F32).T.astype(BF16)

    def scores(j, slot, qslot, q_lo=0):
        kc = k_ref[pl.ds(pl.multiple_of(j * TK, TK), TK), :]
        st = jnp.dot(kc, qt_ref[qslot, :, q_lo:], preferred_element_type=F32)
        s_ref[slot, :, q_lo:TQ] = st
        if q_lo == 0:
            cmax_ref[slot] = jnp.max(st, axis=0, keepdims=True)

    def update(j, slot, diag):
        for c in range(TQ // TQ_SUB):
            q_lo = c * TQ_SUB
            cols = slice(q_lo, q_lo + TQ_SUB)
            nk, masked, k_lo = TK, False, 0
            if diag is not None:
                k_lo = diag * TK
                if k_lo >= q_lo + TQ_SUB:
                    continue
                nk = min(TK, q_lo + TQ_SUB - k_lo)
                masked = k_lo + nk - 1 > q_lo
            st = s_ref[slot, :nk, cols]
            if masked:
                row = lax.broadcasted_iota(jnp.int32, st.shape, 0) + k_lo
                col = lax.broadcasted_iota(jnp.int32, st.shape, 1) + q_lo
                st = jnp.where(row <= col, st, -jnp.inf)
            m_old = m_ref[:, cols]
            cmax = cmax_ref[slot][:, cols] if diag is None else jnp.max(st, axis=0, keepdims=True)
            m_new = jnp.maximum(m_old, cmax)
            alpha = jnp.exp2(m_old - m_new)
            p = jnp.exp2(st - m_new).astype(BF16)
            pv = jnp.dot(vt_ref[j][:, :nk], p, preferred_element_type=F32)
            acc_ref[:, cols] = alpha * acc_ref[:, cols] + pv
            m_ref[:, cols] = m_new

    band = TQ // TK
    assert band == 2 and TILE_GROUP % 2 == 0 and nq % TILE_GROUP == 0

    def tile(i, phase):
        qslot = phase % 2
        m_ref[...] = jnp.full(m_ref.shape, -jnp.inf, F32)
        acc_ref[...] = jnp.zeros(acc_ref.shape, F32)
        first_diag = band * i

        def pair(j):
            scores(j + 1, 1, qslot)
            update(j, 0, None)
            scores(j + 2, 0, qslot)
            update(j + 1, 1, None)

        def body(t, carry):
            for p in range(TILE_GROUP):
                pair(2 * (TILE_GROUP * t + p))
            return carry

        lax.fori_loop(0, i // TILE_GROUP, body, 0)
        for p in range(phase):
            pair(first_diag - 2 * (phase - p))

        for d in range(band):
            if d + 1 < band:
                scores(first_diag + d + 1, (d + 1) % 2, qslot, (d + 1) * TK)
            else:
                nxt = jnp.minimum(i + 1, nq - 1)
                load_qt(nxt, 1 - qslot)
                scores(0, 0, 1 - qslot)
            update(first_diag + d, d % 2, d)

        rows = pl.ds(pl.multiple_of(i * TQ, TQ), TQ)
        o_ref[rows, :] = (acc_ref[:VDIM, :] / acc_ref[VDIM:VDIM + 1, :]).T

    load_qt(0, 0)
    scores(0, 0, 0)

    def tile_group(w, carry):
        for phase in range(TILE_GROUP):
            tile(TILE_GROUP * w + phase, phase)
        return carry

    lax.fori_loop(0, nq // TILE_GROUP, tile_group, 0)


def _attention(q3, k3, vt5):
    b, s, _ = q3.shape
    return pl.pallas_call(
        _attn_kernel,
        grid=(b, MLA_HEADS),
        in_specs=[
            pl.BlockSpec((None, s, QK_PAD), lambda bi, h: (bi, 0, h)),
            pl.BlockSpec((None, s, QK_PAD), lambda bi, h: (bi, 0, h)),
            pl.BlockSpec((None, None, s // TK, V_AUG, TK), lambda bi, h: (bi, h, 0, 0, 0)),
        ],
        out_specs=pl.BlockSpec((None, s, VDIM), lambda bi, h: (bi, 0, h)),
        out_shape=jax.ShapeDtypeStruct((b, s, MLA_HEADS * VDIM), F32),
        scratch_shapes=[
            pltpu.VMEM((2, QK_PAD, TQ), BF16),
            pltpu.VMEM((2, TK, TQ + LANE), F32),
            pltpu.VMEM((2, 1, TQ), F32),
            pltpu.VMEM((1, TQ), F32),
            pltpu.VMEM((V_AUG, TQ), F32),
        ],
        compiler_params=_cparams(("parallel", "parallel")),
        name="mla_attention",
    )(q3, k3, vt5)


def _ret_kernel(rq_ref, rk_ref, rv_ref, rg_ref, c_ref, s_ref, dec_ref, xi_ref, zeta_ref, cd_ref, beta_ref,
                o_ref, state_ref):
    @pl.when(pl.program_id(2) == 0)
    def _():
        state_ref[...] = jnp.zeros(state_ref.shape, F32)

    cos = c_ref[...]
    sin = s_ref[...]
    for g in range(RET_GROUP):
        cols = slice(g * RET_DK, (g + 1) * RET_DK)
        rq = rq_ref[:, cols].astype(F32)
        rk = rk_ref[:, cols].astype(F32)
        v = rv_ref[:, cols]
        q = (rq * cos + pltpu.roll(rq, RET_DK // 2, 1) * sin) * (RET_DK ** -0.5)
        k = rk * cos + pltpu.roll(rk, RET_DK // 2, 1) * sin
        scores = lax.dot_general(q.astype(BF16), k.astype(BF16), (((1,), (1,)), ((), ())),
                                 preferred_element_type=F32) * dec_ref[g]
        st = state_ref[g]
        o = (jnp.dot(scores.astype(BF16), v, preferred_element_type=F32)
             + jnp.dot((q * xi_ref[g]).astype(BF16), st.astype(BF16), preferred_element_type=F32))
        kz = (k * zeta_ref[g]).astype(BF16)
        state_ref[g] = cd_ref[g] * st + lax.dot_general(kz, v, (((0,), (0,)), ((), ())),
                                                        preferred_element_type=F32)
        mu = jnp.mean(o, axis=-1, keepdims=True)
        d = o - mu
        var = jnp.mean(d * d, axis=-1, keepdims=True)
        on = d * lax.rsqrt(var + GN_EPS)
        gate = rg_ref[:, cols].astype(F32)
        silu = gate * (1.0 / (1.0 + jnp.exp(-gate)))
        o_ref[:, cols] = (on * silu * beta_ref[:, cols]).astype(o_ref.dtype)


def _retention(proj3, c128, s128, dec, xi, zeta, cd, beta, l):
    b, s, _ = proj3.shape
    gw = RET_GROUP * RET_DK
    ngroups = RET_HEADS // RET_GROUP
    seg = lambda k: pl.BlockSpec((None, TM_RET, gw),
                                 lambda bi, h, i: (bi, i, (RET_COL0 + k * RET_HEADS) // RET_GROUP + h))
    tab = pl.BlockSpec((None, TM_RET, LANE), lambda bi, h, i: (bi, i, 0))
    per_head = lambda rows, width: pl.BlockSpec((RET_GROUP, rows, width), lambda bi, h, i: (h, 0, 0))
    return pl.pallas_call(
        _ret_kernel,
        grid=(b, ngroups, s // TM_RET),
        in_specs=[seg(0), seg(1), seg(2), seg(3), tab, tab,
                  per_head(TM_RET, TM_RET), per_head(TM_RET, LANE), per_head(TM_RET, LANE), per_head(RET_DK, LANE),
                  pl.BlockSpec((None, 1, gw), lambda bi, h, i: (l, 0, h))],
        out_specs=pl.BlockSpec((None, TM_RET, gw), lambda bi, h, i: (bi, i, h)),
        out_shape=jax.ShapeDtypeStruct((b, s, RET_HEADS * RET_DV), BF16),
        scratch_shapes=[pltpu.VMEM((RET_GROUP, RET_DK, RET_DV), F32)],
        compiler_params=_cparams(("parallel", "parallel", "arbitrary")),
        name="retention",
    )(proj3, proj3, proj3, proj3, c128, s128, dec, xi, zeta, cd, beta)


def _retention_constants():
    h, c = RET_HEADS, TM_RET
    log_gamma = jnp.log1p(-jnp.exp2(-5.0 - jnp.arange(h, dtype=F32)))
    idx = jnp.arange(c, dtype=F32)
    rel = idx[:, None] - idx[None, :]
    dec = jnp.where(rel >= 0, jnp.exp(log_gamma[:, None, None] * jnp.maximum(rel, 0.0)), 0.0)
    xi = jnp.exp(log_gamma[:, None] * (idx + 1.0))
    zeta = jnp.exp(log_gamma[:, None] * (c - 1.0 - idx))
    cd = jnp.exp(log_gamma * c)
    bc = lambda t: jnp.broadcast_to(t[:, :, None], (h, c, LANE))
    return dec, bc(xi), bc(zeta), jnp.broadcast_to(cd[:, None, None], (h, RET_DK, LANE))


def _outproj_kernel(a_ref, r_ref, x_ref, ba_ref, wt_ref, wb_ref, o_ref):
    an = _rms(a_ref[...], ba_ref[...]).astype(BF16)
    y = jnp.dot(an, wt_ref[...], preferred_element_type=F32)
    y = y + jnp.dot(r_ref[...], wb_ref[...], preferred_element_type=F32)
    o_ref[...] = x_ref[...] + y


def _outproj(a2, r2, x2, ba, wo, l):
    t = x2.shape[0]
    half = MLA_HEADS * VDIM
    return pl.pallas_call(
        _outproj_kernel,
        grid=(t // TM_OUT,),
        in_specs=[
            pl.BlockSpec((TM_OUT, half), lambda i: (i, 0)),
            pl.BlockSpec((TM_OUT, half), lambda i: (i, 0)),
            pl.BlockSpec((TM_OUT, D_MODEL), lambda i: (i, 0)),
            pl.BlockSpec((None, 1, half), lambda i: (l, 0, 0)),
            pl.BlockSpec((None, half, D_MODEL), lambda i: (l, 0, 0)),
            pl.BlockSpec((None, half, D_MODEL), lambda i: (l, 1, 0)),
        ],
        out_specs=pl.BlockSpec((TM_OUT, D_MODEL), lambda i: (i, 0)),
        out_shape=jax.ShapeDtypeStruct((t, D_MODEL), F32),
        compiler_params=_cparams(("parallel",)),
        name="out_proj",
    )(a2, r2, x2, ba, wo, wo)


def _mlp_kernel(x_ref, g_ref, wu_ref, wd_ref, fg_ref, o_ref, h_ref, *, final):
    j = pl.program_id(1)

    @pl.when(j == 0)
    def _():
        x = x_ref[...]
        h_ref[...] = _rms(x, g_ref[...]).astype(BF16)
        o_ref[...] = x

    u = jnp.maximum(jnp.dot(h_ref[...], wu_ref[...], preferred_element_type=F32), 0.0)
    u = (u * u).astype(BF16)
    o_ref[...] += jnp.dot(u, wd_ref[...], preferred_element_type=F32)

    if final:
        @pl.when(j == pl.num_programs(1) - 1)
        def _():
            o_ref[...] = _rms(o_ref[...], fg_ref[...])


def _mlp(x2, g, wu, wd, fg, l, final):
    t = x2.shape[0]
    return pl.pallas_call(
        functools.partial(_mlp_kernel, final=final),
        grid=(t // TM_MLP, D_FF // TF_MLP),
        in_specs=[
            pl.BlockSpec((TM_MLP, D_MODEL), lambda i, j: (i, 0)),
            pl.BlockSpec((None, 1, D_MODEL), lambda i, j: (l, 0, 0)),
            pl.BlockSpec((None, D_MODEL, TF_MLP), lambda i, j: (l, 0, j)),
            pl.BlockSpec((None, TF_MLP, D_MODEL), lambda i, j: (l, j, 0)),
            pl.BlockSpec((1, D_MODEL), lambda i, j: (0, 0)),
        ],
        out_specs=pl.BlockSpec((TM_MLP, D_MODEL), lambda i, j: (i, 0)),
        out_shape=jax.ShapeDtypeStruct((t, D_MODEL), F32),
        scratch_shapes=[pltpu.VMEM((TM_MLP, D_MODEL), BF16)],
        compiler_params=_cparams(("parallel", "arbitrary")),
        name="mlp",
    )(x2, g, wu, wd, fg)


def _prep_w_in(w):
    cut = Q_LORA + KV_LORA + ROPE
    pad = jnp.zeros(w.shape[:2] + (MLA_IN - cut,), BF16)
    wm = jnp.concatenate([w[..., :cut].astype(BF16), pad], axis=-1)
    return wm, w[..., cut:].astype(BF16)


def _prep_w_uq(w):
    w4 = w.reshape(DEPTH, Q_LORA, MLA_HEADS, NOPE + ROPE)
    wqn = w4[..., :NOPE].reshape(DEPTH, Q_LORA, MLA_HEADS * NOPE).astype(BF16)
    wqr = w4[..., NOPE:].reshape(DEPTH, Q_LORA, MLA_HEADS * ROPE).astype(BF16)
    return wqn, wqr


def _prep_w_ukv(w):
    w4 = w.reshape(DEPTH, KV_LORA, MLA_HEADS, NOPE + VDIM)
    wk = w4[..., :NOPE].reshape(DEPTH, KV_LORA, MLA_HEADS * NOPE).astype(BF16)
    wvt = jnp.swapaxes(w4[..., NOPE:].reshape(DEPTH, KV_LORA, MLA_HEADS * VDIM), 1, 2).astype(BF16)
    return wk, wvt


def kernel(x, positions, attn_norm, w_in, q_norm, kv_norm, w_uq, w_ukv, beta_attn, beta_ret,
           w_o, mlp_norm, w_up, w_down, final_norm):
    b, s, d = x.shape
    t = b * s
    c128, s128, c64, s64 = _rope_tables(positions)
    c128, s128, c64, s64 = (a.reshape(b, s, LANE) for a in (c128, s128, c64, s64))
    dec, xi, zeta, cd = _retention_constants()

    row = lambda g: g.reshape(DEPTH, 1, -1)
    w_mla, w_ret = _prep_w_in(w_in)
    wqn, wqr = _prep_w_uq(w_uq)
    wk, wvt = _prep_w_ukv(w_ukv)
    w_o_b, w_up_b, w_down_b = w_o.astype(BF16), w_up.astype(BF16), w_down.astype(BF16)
    fg = final_norm.reshape(1, d)

    x2 = x.reshape(t, d)
    for l in range(DEPTH):
        proj3 = _inproj(x2, row(attn_norm), w_mla, w_ret, l).reshape(b, s, PROJ_W)
        q3, k3, vt5 = _mla_up(proj3, row(q_norm), row(kv_norm), wqn, wqr, wk, wvt, c64, s64, l)
        a3 = _attention(q3, k3, vt5)
        r3 = _retention(proj3, c128, s128, dec, xi, zeta, cd, row(beta_ret), l)
        x2 = _outproj(a3.reshape(t, -1), r3.reshape(t, -1), x2, row(beta_attn), w_o_b, l)
        x2 = _mlp(x2, row(mlp_norm), w_up_b, w_down_b, fg, l, final=(l == DEPTH - 1))
    return x2.reshape(b, s, d)
```
